```python
import math
import jax, jax.numpy as jnp
from jax import lax
import numpy as np

D_MODEL = 2048
BATCH = 8
SEQ = 2048
DEPTH = 1
DEC_BATCH = 16
DEC_SEQ = 64
PAST_LEN = 1024

CHUNK = 64
Q_BLOCK = 128
N_HEADS = 8
HEAD_DIM = D_MODEL // N_HEADS // 4
V_DIM = 2 * HEAD_DIM
ATTN_QK_WIDTH = N_HEADS * 2 * HEAD_DIM
ATTN_V_WIDTH = N_HEADS * V_DIM
ROPE_DIM = HEAD_DIM // 4
ROPE_THETA = 500000.0
SCALE = HEAD_DIM ** -0.5
D_CONV = D_MODEL // 2
CONV_W = 31
N_GROUPS = 4
EXPERTS_PER_GROUP = 8
N_EXPERTS = N_GROUPS * EXPERTS_PER_GROUP
TOP_K = 2
D_EXPERT = D_MODEL // 4
EPS = 1e-6

O_Q = 2 * D_CONV
O_K = O_Q + ATTN_QK_WIDTH
O_V = O_K + ATTN_QK_WIDTH
O_G = O_V + ATTN_V_WIDTH
N_IN = O_G + 2 * D_MODEL

kernel_name = "hybrid_conformer_diffattn_hmoe_stream_step"


def rmsnorm(x, g):
    xf = x.astype(jnp.float32)
    y = xf * lax.rsqrt(jnp.mean(xf * xf, axis=-1, keepdims=True) + EPS)
    return (y * g.astype(jnp.float32)).astype(x.dtype)


def layernorm(x, g, b):
    xf = x.astype(jnp.float32)
    mu = jnp.mean(xf, axis=-1, keepdims=True)
    xc = xf - mu
    y = xc * lax.rsqrt(jnp.mean(xc * xc, axis=-1, keepdims=True) + EPS)
    return (y * g.astype(jnp.float32) + b.astype(jnp.float32)).astype(x.dtype)


def partial_rope(x, pos):
    half = ROPE_DIM // 2
    inv_freq = 1.0 / (ROPE_THETA ** (jnp.arange(0, ROPE_DIM, 2, dtype=jnp.float32) / ROPE_DIM))
    ang = pos.astype(jnp.float32)[:, None] * inv_freq[None, :]
    cos = jnp.cos(ang)[None, :, None, None, :]
    sin = jnp.sin(ang)[None, :, None, None, :]
    xr = x[..., :ROPE_DIM].astype(jnp.float32)
    x1, x2 = xr[..., :half], xr[..., half:]
    rot = jnp.concatenate([x1 * cos - x2 * sin, x2 * cos + x1 * sin], axis=-1)
    return jnp.concatenate([rot.astype(x.dtype), x[..., ROPE_DIM:]], axis=-1)


def diff_core(q, k, v, lam, mask):
    s = jnp.einsum('bqhmd,bkhmd->bhmqk', q, k, preferred_element_type=jnp.float32) * SCALE
    if mask is not None:
        s = jnp.where(mask[None, None, None], s, -jnp.inf)
    p = jax.nn.softmax(s, axis=-1)
    a = p[:, :, 0] - lam * p[:, :, 1]
    return jnp.einsum('bhqk,bkhe->bqhe', a.astype(v.dtype), v)


def prompt_diff_attention(q, k, v, lam):
    B, T = q.shape[0], q.shape[1]
    nb = T // Q_BLOCK
    qb = jnp.moveaxis(q.reshape(B, nb, Q_BLOCK, N_HEADS, 2, HEAD_DIM), 1, 0)
    k_chunk = jnp.arange(T) // CHUNK

    def block(args):
        i, qi = args
        q_chunk = (i * Q_BLOCK + jnp.arange(Q_BLOCK)) // CHUNK
        return diff_core(qi, k, v, lam, k_chunk[None, :] <= q_chunk[:, None])

    ob = lax.map(block, (jnp.arange(nb), qb))
    return jnp.moveaxis(ob, 0, 1).reshape(B, T, N_HEADS, V_DIM)


def token_mixer(xn, conv_hist, k_past, v_past, pos, lam, lam_init, w_in, b_glu, w_dw, b_dw,
                ln_g, ln_b, w_co, b_co, subln_g, w_ao, w_out):
    B, T, _ = xn.shape
    z = xn @ w_in
    a = z[..., :O_Q] + b_glu
    glu = a[..., :D_CONV] * jax.nn.sigmoid(a[..., D_CONV:])
    q = partial_rope(z[..., O_Q:O_K].reshape(B, T, N_HEADS, 2, HEAD_DIM), pos)
    k = partial_rope(z[..., O_K:O_V].reshape(B, T, N_HEADS, 2, HEAD_DIM), pos)
    v = z[..., O_V:O_G].reshape(B, T, N_HEADS, V_DIM)
    gates = jax.nn.sigmoid(z[..., O_G:])
    g_conv, g_attn = gates[..., :D_MODEL], gates[..., D_MODEL:]

    if conv_hist is None:
        conv_hist = jnp.zeros((B, CONV_W - 1, D_CONV), glu.dtype)
    xp = jnp.concatenate([conv_hist.astype(glu.dtype), glu], axis=1)
    c = lax.conv_general_dilated(xp, w_dw[:, None, :].astype(xp.dtype), window_strides=(1,),
                                 padding='VALID', dimension_numbers=('NWC', 'WIO', 'NWC'),
                                 feature_group_count=D_CONV) + b_dw
    c = jax.nn.silu(layernorm(c, ln_g, ln_b))
    conv_out = c @ w_co + b_co
    new_hist = xp[:, -(CONV_W - 1):]

    if k_past is None:
        o = prompt_diff_attention(q, k, v, lam)
    else:
        kk = jnp.concatenate([k_past.astype(k.dtype), k], axis=1)
        vv = jnp.concatenate([v_past.astype(v.dtype), v], axis=1)
        o = diff_core(q, kk, vv, lam, None)
    o = rmsnorm(o, subln_g) * (1.0 - lam_init)
    attn_out = o.reshape(B, T, N_HEADS * V_DIM) @ w_ao

    merged = g_conv * conv_out + g_attn * attn_out
    return merged @ w_out, new_hist, k, v


def hier_moe(x2d, w_rg, b_rg, w_re, b_re, w_eg, w_eu, w_ed):
    T = x2d.shape[0]
    pg = jax.nn.softmax((x2d @ w_rg).astype(jnp.float32) + b_rg.astype(jnp.float32), axis=-1)
    g_idx = jnp.argmax(pg, axis=-1)
    p_g = jnp.take_along_axis(pg, g_idx[:, None], axis=-1)
    le = ((x2d @ w_re).astype(jnp.float32) + b_re.astype(jnp.float32)).reshape(T, N_GROUPS, EXPERTS_PER_GROUP)
    le_g = jnp.take_along_axis(le, g_idx[:, None, None], axis=1)[:, 0]
    top_v, top_i = lax.top_k(le_g, TOP_K)
    w = jax.nn.softmax(top_v, axis=-1) * p_g
    e_idx = g_idx[:, None] * EXPERTS_PER_GROUP + top_i
    comb = jnp.sum(jax.nn.one_hot(e_idx, N_EXPERTS, dtype=jnp.float32) * w[..., None], axis=1)
    out = jnp.zeros((T, D_MODEL), jnp.float32)
    for e in range(N_EXPERTS):
        h = jax.nn.silu(x2d @ w_eg[e]) * (x2d @ w_eu[e])
        out = out + ((comb[:, e:e + 1].astype(h.dtype) * h) @ w_ed[e]).astype(jnp.float32)
    return out.astype(x2d.dtype)


def setup_inputs(seed: int = 0) -> dict:
    key = jax.random.key(seed)
    ks = jax.random.split(key, 32)
    f32 = jnp.float32

    def nrm(k, shape, scale):
        return jax.random.normal(k, shape, f32) * scale

    L = DEPTH
    return {
        "x_prompt": nrm(ks[0], (BATCH, SEQ, D_MODEL), 1.0),
        "x_sample": nrm(ks[1], (DEC_BATCH, DEC_SEQ, D_MODEL), 1.0),
        "cache_k": nrm(ks[2], (L, DEC_BATCH, PAST_LEN, N_HEADS, 2, HEAD_DIM), 1.0),
        "cache_v": nrm(ks[3], (L, DEC_BATCH, PAST_LEN, N_HEADS, V_DIM), 1.0),
        "state_conv": nrm(ks[4], (L, DEC_BATCH, CONV_W - 1, D_CONV), 0.5),
        "norm1_g": 1.0 + nrm(ks[5], (L, D_MODEL), 0.02),
        "w_in": nrm(ks[6], (L, D_MODEL, N_IN), D_MODEL ** -0.5),
        "b_glu": nrm(ks[7], (L, 2 * D_CONV), 0.02),
        "w_dw": nrm(ks[8], (L, CONV_W, D_CONV), CONV_W ** -0.5),
        "b_dw": nrm(ks[9], (L, D_CONV), 0.02),
        "conv_ln_g": 1.0 + nrm(ks[10], (L, D_CONV), 0.02),
        "conv_ln_b": nrm(ks[11], (L, D_CONV), 0.02),
        "w_conv_out": nrm(ks[12], (L, D_CONV, D_MODEL), D_CONV ** -0.5),
        "b_conv_out": nrm(ks[13], (L, D_MODEL), 0.02),
        "lambda_q1": nrm(ks[14], (L, HEAD_DIM), 0.1),
        "lambda_k1": nrm(ks[15], (L, HEAD_DIM), 0.1),
        "lambda_q2": nrm(ks[16], (L, HEAD_DIM), 0.1),
        "lambda_k2": nrm(ks[17], (L, HEAD_DIM), 0.1),
        "subln_g": 1.0 + nrm(ks[18], (L, V_DIM), 0.02),
        "w_attn_out": nrm(ks[19], (L, ATTN_V_WIDTH, D_MODEL), ATTN_V_WIDTH ** -0.5),
        "w_out": nrm(ks[20], (L, D_MODEL, D_MODEL), D_MODEL ** -0.5),
        "norm2_g": 1.0 + nrm(ks[21], (L, D_MODEL), 0.02),
        "w_router_group": nrm(ks[22], (L, D_MODEL, N_GROUPS), D_MODEL ** -0.5),
        "b_router_group": nrm(ks[23], (L, N_GROUPS), 0.01),
        "w_router_expert": nrm(ks[24], (L, D_MODEL, N_EXPERTS), D_MODEL ** -0.5),
        "b_router_expert": nrm(ks[25], (L, N_EXPERTS), 0.01),
        "w_exp_gate": nrm(ks[26], (L, N_EXPERTS, D_MODEL, D_EXPERT), D_MODEL ** -0.5),
        "w_exp_up": nrm(ks[27], (L, N_EXPERTS, D_MODEL, D_EXPERT), D_MODEL ** -0.5),
        "w_exp_down": nrm(ks[28], (L, N_EXPERTS, D_EXPERT, D_MODEL), D_EXPERT ** -0.5),
        "final_norm_g": 1.0 + nrm(ks[29], (D_MODEL,), 0.02),
    }


def reference(x_prompt, x_sample, cache_k, cache_v, state_conv, norm1_g, w_in, b_glu, w_dw, b_dw,
              conv_ln_g, conv_ln_b, w_conv_out, b_conv_out, lambda_q1, lambda_k1, lambda_q2, lambda_k2,
              subln_g, w_attn_out, w_out, norm2_g, w_router_group, b_router_group, w_router_expert,
              b_router_expert, w_exp_gate, w_exp_up, w_exp_down, final_norm_g):
    Bp, Tp, _ = x_prompt.shape
    Bs, Ts, _ = x_sample.shape
    past = cache_k.shape[2]
    pos_p = jnp.arange(Tp)
    pos_s = past + jnp.arange(Ts)
    xp, xs = x_prompt, x_sample
    kp_l, vp_l, cp_l, ks_l, vs_l, cs_l = [], [], [], [], [], []
    for l in range(DEPTH):
        lam_init = 0.8 - 0.6 * math.exp(-0.3 * l)
        f = jnp.float32
        lam = (jnp.exp(jnp.sum(lambda_q1[l].astype(f) * lambda_k1[l].astype(f)))
               - jnp.exp(jnp.sum(lambda_q2[l].astype(f) * lambda_k2[l].astype(f))) + lam_init)
        mix_w = (w_in[l], b_glu[l], w_dw[l], b_dw[l], conv_ln_g[l], conv_ln_b[l], w_conv_out[l],
                 b_conv_out[l], subln_g[l], w_attn_out[l], w_out[l])
        mp, cp, kp, vp = token_mixer(rmsnorm(xp, norm1_g[l]), None, None, None, pos_p, lam, lam_init, *mix_w)
        ms, cs, ksn, vsn = token_mixer(rmsnorm(xs, norm1_g[l]), state_conv[l], cache_k[l], cache_v[l],
                                       pos_s, lam, lam_init, *mix_w)
        xp = xp + mp
        xs = xs + ms
        tok = jnp.concatenate([rmsnorm(xp, norm2_g[l]).reshape(Bp * Tp, D_MODEL),
                               rmsnorm(xs, norm2_g[l]).reshape(Bs * Ts, D_MODEL)], axis=0)
        ff = hier_moe(tok, w_router_group[l], b_router_group[l], w_router_expert[l], b_router_expert[l],
                      w_exp_gate[l], w_exp_up[l], w_exp_down[l])
        xp = xp + ff[:Bp * Tp].reshape(Bp, Tp, D_MODEL)
        xs = xs + ff[Bp * Tp:].reshape(Bs, Ts, D_MODEL)
        kp_l.append(kp); vp_l.append(vp); cp_l.append(cp)
        ks_l.append(ksn); vs_l.append(vsn); cs_l.append(cs)
    y_prompt = rmsnorm(xp, final_norm_g)
    y_sample = rmsnorm(xs, final_norm_g)
    return (y_prompt, y_sample, jnp.stack(kp_l), jnp.stack(vp_l), jnp.stack(cp_l),
            jnp.stack(ks_l), jnp.stack(vs_l), jnp.stack(cs_l))
```

```python
import functools
import math

import jax
import jax.numpy as jnp
from jax import lax
from jax.experimental import pallas as pl
from jax.experimental.pallas import tpu as pltpu

F32 = jnp.float32
BF16 = jnp.bfloat16
I32 = jnp.int32

EPS = 1e-6
CHUNK = 64
ROPE_DIM = 16
ROPE_THETA = 500000.0
LANES = 128
HIST_PAD = 32
VMEM_LIMIT = 56 * 1024 * 1024


def _cparams(sem):
    return pltpu.CompilerParams(dimension_semantics=sem, vmem_limit_bytes=VMEM_LIMIT)


def _full(shape):
    return pl.BlockSpec(shape, lambda *_: (0,) * len(shape))


def _rms_rows(x_ref, g_ref, xn_ref):
    tm = x_ref.shape[0]
    rc = min(tm, 128)

    def body(r, carry):
        r0 = pl.multiple_of(r * rc, rc)
        xf = x_ref[pl.ds(r0, rc), :]
        ms = jnp.mean(xf * xf, axis=-1, keepdims=True)
        y = xf * lax.rsqrt(ms + EPS) * g_ref[...]
        xn_ref[pl.ds(r0, rc), :] = y.astype(xn_ref.dtype)
        return carry

    lax.fori_loop(0, tm // rc, body, 0)


def _rope(z, c_ref, s1_ref, s2_ref):
    outs = []
    for c in range(z.shape[1] // LANES):
        zc = z[:, c * LANES:(c + 1) * LANES]
        up = pltpu.roll(zc, LANES - ROPE_DIM // 2, axis=1)
        dn = pltpu.roll(zc, ROPE_DIM // 2, axis=1)
        outs.append(zc * c_ref[...] + up * s1_ref[...] + dn * s2_ref[...])
    return outs[0] if len(outs) == 1 else jnp.concatenate(outs, axis=1)


def _in_glu_kernel(x_ref, g_ref, wa_ref, wb_ref, ba_ref, bb_ref, o_ref, xn_ref):
    @pl.when(pl.program_id(1) == 0)
    def _():
        _rms_rows(x_ref, g_ref, xn_ref)

    xn = xn_ref[...]
    a = jnp.dot(xn, wa_ref[...], preferred_element_type=F32) + ba_ref[...]
    b = jnp.dot(xn, wb_ref[...], preferred_element_type=F32) + bb_ref[...]
    o_ref[...] = a * jax.nn.sigmoid(b)


def _in_rope_kernel(scale, x_ref, g_ref, w_ref, c_ref, s1_ref, s2_ref, o_ref, xn_ref):
    @pl.when(pl.program_id(1) == 0)
    def _():
        _rms_rows(x_ref, g_ref, xn_ref)

    z = jnp.dot(xn_ref[...], w_ref[...], preferred_element_type=F32)
    z = _rope(z, c_ref, s1_ref, s2_ref)
    if scale != 1.0:
        z = z * scale
    o_ref[...] = z.astype(o_ref.dtype)


def _in_plain_kernel(act, x_ref, g_ref, w_ref, o_ref, xn_ref):
    @pl.when(pl.program_id(1) == 0)
    def _():
        _rms_rows(x_ref, g_ref, xn_ref)

    z = jnp.dot(xn_ref[...], w_ref[...], preferred_element_type=F32)
    if act == "sigmoid":
        z = jax.nn.sigmoid(z)
    o_ref[...] = z.astype(o_ref.dtype)


def _in_proj(mode, x2, g, w_bf, col0, ncols, tm, tn, out_dtype, extra=()):
    n, d = x2.shape
    grid = (n // tm, ncols // tn)
    cb0 = col0 // tn
    x_spec = pl.BlockSpec((tm, d), lambda i, j: (i, 0))
    g_spec = _full((1, d))
    w_spec = pl.BlockSpec((d, tn), lambda i, j: (0, cb0 + j))
    o_spec = pl.BlockSpec((tm, tn), lambda i, j: (i, j))
    scratch = [pltpu.VMEM((tm, d), BF16)]
    if mode == "glu":
        b_glu, = extra
        half = ncols
        wb_spec = pl.BlockSpec((d, tn), lambda i, j: (0, cb0 + half // tn + j))
        ba_spec = pl.BlockSpec((1, tn), lambda i, j: (0, j))
        bb_spec = pl.BlockSpec((1, tn), lambda i, j: (0, half // tn + j))
        return pl.pallas_call(
            _in_glu_kernel, grid=grid,
            in_specs=[x_spec, g_spec, w_spec, wb_spec, ba_spec, bb_spec],
            out_specs=o_spec, out_shape=jax.ShapeDtypeStruct((n, ncols), out_dtype),
            scratch_shapes=scratch, compiler_params=_cparams(("parallel", "arbitrary")),
            name="in_glu")(x2, g, w_bf, w_bf, b_glu, b_glu)
    if mode == "rope":
        tabs, scale = extra
        nper = tabs[0].shape[0] // tm
        t_spec = pl.BlockSpec((tm, LANES), lambda i, j: (i % nper, 0))
        return pl.pallas_call(
            functools.partial(_in_rope_kernel, scale), grid=grid,
            in_specs=[x_spec, g_spec, w_spec, t_spec, t_spec, t_spec],
            out_specs=o_spec, out_shape=jax.ShapeDtypeStruct((n, ncols), out_dtype),
            scratch_shapes=scratch, compiler_params=_cparams(("parallel", "arbitrary")),
            name="in_rope")(x2, g, w_bf, *tabs)
    act, = extra
    return pl.pallas_call(
        functools.partial(_in_plain_kernel, act), grid=grid,
        in_specs=[x_spec, g_spec, w_spec],
        out_specs=o_spec, out_shape=jax.ShapeDtypeStruct((n, ncols), out_dtype),
        scratch_shapes=scratch, compiler_params=_cparams(("parallel", "arbitrary")),
        name="in_" + act)(x2, g, w_bf)


def _rope_tables(pos, rows):
    half = ROPE_DIM // 2
    inv_freq = 1.0 / (ROPE_THETA ** (jnp.arange(0, ROPE_DIM, 2, dtype=F32) / ROPE_DIM))
    ang = pos.astype(F32)[:, None] * inv_freq[None, :]
    cos, sin = jnp.cos(ang), jnp.sin(ang)
    t = pos.shape[0]
    sub = 64
    ones = jnp.ones((t, sub - ROPE_DIM), F32)
    zeros = jnp.zeros((t, sub - ROPE_DIM), F32)
    zh = jnp.zeros((t, half), F32)
    c = jnp.concatenate([cos, cos, ones], axis=1)
    s1 = jnp.concatenate([-sin, zh, zeros], axis=1)
    s2 = jnp.concatenate([zh, sin, zeros], axis=1)
    reps = (rows // t, LANES // sub)
    return tuple(jnp.tile(a, reps) for a in (c, s1, s2))


def _conv_kernel(conv_w, glu_ref, hist_ref, wdw_ref, bdw_ref, lng_ref, lnb_ref, wco_ref, bco_ref,
                 gate_ref, o_ref, xp_ref, c_ref):
    tt = glu_ref.shape[1]
    i = pl.program_id(1)

    @pl.when(i == 0)
    def _():
        xp_ref[0:HIST_PAD, :] = hist_ref[0]

    @pl.when(i > 0)
    def _():
        xp_ref[0:HIST_PAD, :] = xp_ref[tt:tt + HIST_PAD, :]

    xp_ref[HIST_PAD:HIST_PAD + tt, :] = glu_ref[0]

    rc = 16
    off = HIST_PAD - (conv_w - 1)
    for r in range(tt // rc):
        acc = xp_ref[pl.ds(r * rc + off, rc), :] * wdw_ref[0:1, :]
        for j in range(1, conv_w):
            acc = acc + xp_ref[pl.ds(r * rc + off + j, rc), :] * wdw_ref[j:j + 1, :]
        acc = acc + bdw_ref[...]
        mu = jnp.mean(acc, axis=-1, keepdims=True)
        xc = acc - mu
        var = jnp.mean(xc * xc, axis=-1, keepdims=True)
        y = xc * lax.rsqrt(var + EPS) * lng_ref[...] + lnb_ref[...]
        y = y * jax.nn.sigmoid(y)
        c_ref[r * rc:(r + 1) * rc, :] = y.astype(BF16)

    out = jnp.dot(c_ref[...], wco_ref[...], preferred_element_type=F32) + bco_ref[...]
    o_ref[0] = (gate_ref[0].astype(F32) * out).astype(o_ref.dtype)


def _conv_branch(glu3, hist, w_dw, b_dw, ln_g, ln_b, w_co_bf, b_co, gates3, tt):
    b, t, c = glu3.shape
    d = w_co_bf.shape[1]
    conv_w = w_dw.shape[0]
    return pl.pallas_call(
        functools.partial(_conv_kernel, conv_w), grid=(b, t // tt),
        in_specs=[pl.BlockSpec((1, tt, c), lambda bi, i: (bi, i, 0)),
                  pl.BlockSpec((1, HIST_PAD, c), lambda bi, i: (bi, 0, 0)),
                  _full((conv_w, c)), _full((1, c)), _full((1, c)), _full((1, c)),
                  _full((c, d)), _full((1, d)),
                  pl.BlockSpec((1, tt, d), lambda bi, i: (bi, i, 0))],
        out_specs=pl.BlockSpec((1, tt, d), lambda bi, i: (bi, i, 0)),
        out_shape=jax.ShapeDtypeStruct((b, t, d), BF16),
        scratch_shapes=[pltpu.VMEM((HIST_PAD + tt, c), F32), pltpu.VMEM((tt, c), BF16)],
        compiler_params=_cparams(("parallel", "arbitrary")),
        name="conv_branch")(glu3, hist, w_dw, b_dw, ln_g, ln_b, w_co_bf, b_co, gates3)


def _lambda(lam_ref, lam_init):
    lq1, lk1, lq2, lk2 = (lam_ref[r:r + 1, :] for r in range(4))
    return (jnp.exp(jnp.sum(lq1 * lk1, axis=-1, keepdims=True))
            - jnp.exp(jnp.sum(lq2 * lk2, axis=-1, keepdims=True)) + lam_init)


def _subln(o, g_ref, lam_init):
    y = o * lax.rsqrt(jnp.mean(o * o, axis=-1, keepdims=True) + EPS)
    return y * g_ref[...] * (1.0 - lam_init)


def _qk(q, k):
    return lax.dot_general(q, k, (((1,), (1,)), ((), ())), preferred_element_type=F32)


def _attn_prompt_kernel(lam_init, q_ref, k_ref, v_ref, lam_ref, sg_ref, o_ref, kb_ref, vb_ref):
    tq = q_ref.shape[1]
    hd = q_ref.shape[2] // 2
    qi = pl.program_id(2)

    @pl.when(qi == 0)
    def _():
        kb_ref[...] = k_ref[0].astype(BF16)
        vb_ref[...] = v_ref[0].astype(BF16)

    q = q_ref[0]
    qs = (q[:, :hd], q[:, hd:])

    def step(k0, state, masked):
        kblk = kb_ref[pl.ds(k0, tq), :]
        vblk = vb_ref[pl.ds(k0, tq), :]
        new = []
        for m in range(2):
            mx, l, acc = state[m]
            s = _qk(qs[m], kblk[:, m * hd:(m + 1) * hd])
            if masked:
                qc = lax.broadcasted_iota(I32, (tq, tq), 0) // CHUNK
                kc = lax.broadcasted_iota(I32, (tq, tq), 1) // CHUNK
                s = jnp.where(kc <= qc, s, -jnp.inf)
            mn = jnp.maximum(mx, jnp.max(s, axis=-1, keepdims=True))
            alpha = jnp.exp(mx - mn)
            p = jnp.exp(s - mn)
            l = alpha * l + jnp.sum(p, axis=-1, keepdims=True)
            acc = alpha * acc + jnp.dot(p.astype(BF16), vblk, preferred_element_type=F32)
            new.append((mn, l, acc))
        return tuple(new)

    init = tuple((jnp.full((tq, 1), -jnp.inf, F32), jnp.zeros((tq, 1), F32),
                  jnp.zeros((tq, v_ref.shape[2]), F32)) for _ in range(2))
    state = lax.fori_loop(0, qi, lambda kb, st: step(pl.multiple_of(kb * tq, tq), st, False), init)
    state = step(pl.multiple_of(qi * tq, tq), state, True)
    lam = _lambda(lam_ref, lam_init)
    o = state[0][2] / state[0][1] - lam * (state[1][2] / state[1][1])
    o_ref[0] = _subln(o, sg_ref, lam_init).astype(o_ref.dtype)


def _attn_prompt(q3, k3, v3, lam4, subln_g, lam_init, n_heads, tq):
    b, t, w = q3.shape
    hw = w // n_heads
    blk = lambda bi, h, i: (bi, i, h)
    whole = lambda bi, h, i: (bi, 0, h)
    return pl.pallas_call(
        functools.partial(_attn_prompt_kernel, lam_init), grid=(b, n_heads, t // tq),
        in_specs=[pl.BlockSpec((1, tq, hw), blk), pl.BlockSpec((1, t, hw), whole),
                  pl.BlockSpec((1, t, hw), whole), _full(lam4.shape), _full(subln_g.shape)],
        out_specs=pl.BlockSpec((1, tq, hw), blk),
        out_shape=jax.ShapeDtypeStruct((b, t, w), BF16),
        scratch_shapes=[pltpu.VMEM((t, hw), BF16), pltpu.VMEM((t, hw), BF16)],
        compiler_params=_cparams(("parallel", "parallel", "arbitrary")),
        name="attn_prompt")(q3, k3, v3, lam4, subln_g)


def _attn_sample_kernel(lam_init, q_ref, ck_ref, cv_ref, kn_ref, vn_ref, lam_ref, sg_ref, o_ref):
    hd = q_ref.shape[2] // 2
    q = q_ref[0]
    ck = ck_ref[0].astype(BF16)
    cv = cv_ref[0].astype(BF16)
    kn = kn_ref[0].astype(BF16)
    vn = vn_ref[0].astype(BF16)
    outs = []
    for m in range(2):
        qm = q[:, m * hd:(m + 1) * hd]
        sp = _qk(qm, ck[:, m * hd:(m + 1) * hd])
        sn = _qk(qm, kn[:, m * hd:(m + 1) * hd])
        mx = jnp.maximum(jnp.max(sp, axis=-1, keepdims=True), jnp.max(sn, axis=-1, keepdims=True))
        pp = jnp.exp(sp - mx)
        pn = jnp.exp(sn - mx)
        l = jnp.sum(pp, axis=-1, keepdims=True) + jnp.sum(pn, axis=-1, keepdims=True)
        acc = (jnp.dot(pp.astype(BF16), cv, preferred_element_type=F32)
               + jnp.dot(pn.astype(BF16), vn, preferred_element_type=F32))
        outs.append(acc / l)
    lam = _lambda(lam_ref, lam_init)
    o = outs[0] - lam * outs[1]
    o_ref[0] = _subln(o, sg_ref, lam_init).astype(o_ref.dtype)


def _attn_sample(q3, ck3, cv3, kn3, vn3, lam4, subln_g, lam_init, n_heads):
    b, t, w = q3.shape
    past = ck3.shape[1]
    hw = w // n_heads
    idx = lambda bi, h: (bi, 0, h)
    return pl.pallas_call(
        functools.partial(_attn_sample_kernel, lam_init), grid=(b, n_heads),
        in_specs=[pl.BlockSpec((1, t, hw), idx), pl.BlockSpec((1, past, hw), idx),
                  pl.BlockSpec((1, past, hw), idx), pl.BlockSpec((1, t, hw), idx),
                  pl.BlockSpec((1, t, hw), idx), _full(lam4.shape), _full(subln_g.shape)],
        out_specs=pl.BlockSpec((1, t, hw), idx),
        out_shape=jax.ShapeDtypeStruct((b, t, w), BF16),
        compiler_params=_cparams(("parallel", "parallel")),
        name="attn_sample")(q3, ck3, cv3, kn3, vn3, lam4, subln_g)


def _mid_kernel(n_groups, epg, x_ref, gc_ref, ga_ref, o_ref, wao_ref, wout_ref, g2_ref, wr_ref, br_ref,
                x1_ref, xn2_ref, route_ref):
    attn = jnp.dot(o_ref[...], wao_ref[...], preferred_element_type=F32)
    merged = gc_ref[...].astype(F32) + ga_ref[...].astype(F32) * attn
    x1 = x_ref[...] + jnp.dot(merged.astype(BF16), wout_ref[...], preferred_element_type=F32)
    x1_ref[...] = x1
    xn2 = x1 * lax.rsqrt(jnp.mean(x1 * x1, axis=-1, keepdims=True) + EPS) * g2_ref[...]
    xn2_ref[...] = xn2

    n_exp = n_groups * epg
    lg = jnp.dot(xn2, wr_ref[...], preferred_element_type=F32, precision=lax.Precision.HIGHEST) + br_ref[...]
    lane = lax.broadcasted_iota(I32, lg.shape, 1)
    neg = -jnp.inf
    big = jnp.int32(2 ** 30)
    is_g = (lane >= n_exp) & (lane < n_exp + n_groups)
    gl = jnp.where(is_g, lg, neg)
    gmax = jnp.max(gl, axis=-1, keepdims=True)
    g_idx = jnp.min(jnp.where(gl == gmax, lane, big), axis=-1, keepdims=True) - n_exp
    p_g = 1.0 / jnp.sum(jnp.where(is_g, jnp.exp(gl - gmax), 0.0), axis=-1, keepdims=True)
    in_grp = (lane >= g_idx * epg) & (lane < (g_idx + 1) * epg)
    el = jnp.where(in_grp, lg, neg)
    v1 = jnp.max(el, axis=-1, keepdims=True)
    e1 = jnp.min(jnp.where(el == v1, lane, big), axis=-1, keepdims=True)
    el2 = jnp.where(lane == e1, neg, el)
    v2 = jnp.max(el2, axis=-1, keepdims=True)
    e2 = jnp.min(jnp.where(el2 == v2, lane, big), axis=-1, keepdims=True)
    t2 = jnp.exp(v2 - v1)
    den = 1.0 + t2
    w1 = p_g / den
    w2 = p_g * t2 / den
    route = jnp.where(lane == 0, e1.astype(F32),
                      jnp.where(lane == 1, e2.astype(F32),
                                jnp.where(lane == 2, w1, jnp.where(lane == 3, w2, 0.0))))
    route_ref[...] = route


def _mid(x2, gc2, gates2, o2, w_ao_bf, w_out_bf, g2, wr, br, n_groups, epg, tm):
    n, d = x2.shape
    wo = o2.shape[1]
    row = lambda i: (i, 0)
    return pl.pallas_call(
        functools.partial(_mid_kernel, n_groups, epg), grid=(n // tm,),
        in_specs=[pl.BlockSpec((tm, d), row), pl.BlockSpec((tm, d), row),
                  pl.BlockSpec((tm, d), lambda i: (i, 1)), pl.BlockSpec((tm, wo), row),
                  _full(w_ao_bf.shape), _full(w_out_bf.shape), _full((1, d)), _full(wr.shape), _full(br.shape)],
        out_specs=[pl.BlockSpec((tm, d), row), pl.BlockSpec((tm, d), row), pl.BlockSpec((tm, LANES), row)],
        out_shape=[jax.ShapeDtypeStruct((n, d), F32), jax.ShapeDtypeStruct((n, d), F32),
                   jax.ShapeDtypeStruct((n, LANES), F32)],
        compiler_params=_cparams(("parallel",)),
        name="mid")(x2, gc2, gates2, o2, w_ao_bf, w_out_bf, g2, wr, br)


def _dispatch_kernel(slot_ref, xn_hbm, xs_in_hbm, xs_hbm, sem):
    del xs_in_hbm
    tmd = slot_ref.shape[2] // 2
    base = pl.program_id(0) * tmd

    def row_copy(t, s):
        return pltpu.make_async_copy(xn_hbm.at[pl.ds(t, 1)], xs_hbm.at[pl.ds(s, 1)], sem)

    def issue(r, carry):
        row_copy(base + r, slot_ref[0, 0, 2 * r]).start()
        row_copy(base + r, slot_ref[0, 0, 2 * r + 1]).start()
        return carry

    def drain(r, carry):
        row_copy(0, 0).wait()
        row_copy(0, 0).wait()
        return carry

    lax.fori_loop(0, tmd, issue, 0)
    lax.fori_loop(0, tmd, drain, 0)


def _dispatch(slots, xn2, xs, tmd):
    n, d = xn2.shape
    nt = n // tmd
    return pl.pallas_call(
        _dispatch_kernel, grid=(nt,),
        in_specs=[pl.BlockSpec((1, 1, 2 * tmd), lambda i: (i, 0, 0), memory_space=pltpu.SMEM),
                  pl.BlockSpec(memory_space=pl.ANY), pl.BlockSpec(memory_space=pl.ANY)],
        out_specs=pl.BlockSpec(memory_space=pl.ANY),
        out_shape=jax.ShapeDtypeStruct(xs.shape, xs.dtype),
        scratch_shapes=[pltpu.SemaphoreType.DMA],
        input_output_aliases={2: 0},
        compiler_params=_cparams(("arbitrary",)),
        name="dispatch")(slots.reshape(nt, 1, 2 * tmd), xn2, xs)


def _expert_kernel(te_ref, nt_ref, xs_ref, wg_ref, wu_ref, wd_ref, ys_ref):
    del te_ref
    i = pl.program_id(0)

    @pl.when(i < nt_ref[0])
    def _():
        x = xs_ref[...].astype(BF16)
        hg = jnp.dot(x, wg_ref[0], preferred_element_type=F32)
        hu = jnp.dot(x, wu_ref[0], preferred_element_type=F32)
        h = (hg * jax.nn.sigmoid(hg)) * hu
        ys_ref[...] = jnp.dot(h.astype(BF16), wd_ref[0], preferred_element_type=F32)

    @pl.when(i >= nt_ref[0])
    def _():
        ys_ref[...] = jnp.zeros_like(ys_ref)


def _experts(tile_expert, n_tiles, xs, wg_bf, wu_bf, wd_bf, tme):
    s, d = xs.shape
    de = wg_bf.shape[2]
    grid_spec = pltpu.PrefetchScalarGridSpec(
        num_scalar_prefetch=2, grid=(s // tme,),
        in_specs=[pl.BlockSpec((tme, d), lambda i, te, nt: (i, 0)),
                  pl.BlockSpec((1, d, de), lambda i, te, nt: (te[i], 0, 0)),
                  pl.BlockSpec((1, d, de), lambda i, te, nt: (te[i], 0, 0)),
                  pl.BlockSpec((1, de, d), lambda i, te, nt: (te[i], 0, 0))],
        out_specs=pl.BlockSpec((tme, d), lambda i, te, nt: (i, 0)))
    return pl.pallas_call(
        _expert_kernel, grid_spec=grid_spec,
        out_shape=jax.ShapeDtypeStruct((s, d), F32),
        compiler_params=_cparams(("arbitrary",)),
        name="experts")(tile_expert, n_tiles, xs, wg_bf, wu_bf, wd_bf)


def _combine_kernel(slot_ref, x1_ref, route_ref, gf_ref, ys_hbm, y_ref, buf_ref, sem):
    tmf = x1_ref.shape[0]

    def row_copy(s, k, r):
        return pltpu.make_async_copy(ys_hbm.at[pl.ds(s, 1)], buf_ref.at[k, pl.ds(r, 1)], sem)

    def issue(r, carry):
        row_copy(slot_ref[0, 0, 2 * r], 0, r).start()
        row_copy(slot_ref[0, 0, 2 * r + 1], 1, r).start()
        return carry

    def drain(r, carry):
        row_copy(0, 0, 0).wait()
        row_copy(0, 0, 0).wait()
        return carry

    lax.fori_loop(0, tmf, issue, 0)
    lax.fori_loop(0, tmf, drain, 0)
    w1 = route_ref[:, 2:3]
    w2 = route_ref[:, 3:4]
    x2 = x1_ref[...] + (w1 * buf_ref[0] + w2 * buf_ref[1])
    y = x2 * lax.rsqrt(jnp.mean(x2 * x2, axis=-1, keepdims=True) + EPS) * gf_ref[...]
    y_ref[...] = y


def _combine(slots, x1, route, gf, ys, tmf):
    n, d = x1.shape
    nt = n // tmf
    row = lambda i: (i, 0)
    return pl.pallas_call(
        _combine_kernel, grid=(nt,),
        in_specs=[pl.BlockSpec((1, 1, 2 * tmf), lambda i: (i, 0, 0), memory_space=pltpu.SMEM),
                  pl.BlockSpec((tmf, d), row), pl.BlockSpec((tmf, LANES), row), _full((1, d)),
                  pl.BlockSpec(memory_space=pl.ANY)],
        out_specs=pl.BlockSpec((tmf, d), row),
        out_shape=jax.ShapeDtypeStruct((n, d), F32),
        scratch_shapes=[pltpu.VMEM((2, tmf, d), F32), pltpu.SemaphoreType.DMA],
        compiler_params=_cparams(("arbitrary",)),
        name="combine")(slots.reshape(nt, 1, 2 * tmf), x1, route, gf, ys)


def _pick(n, pref):
    t = min(n, pref)
    while n % t:
        t //= 2
    return t


def _mixer(x3, hist, k_past, v_past, pos, p, n_heads, lam_init):
    b, t, d = x3.shape
    n = b * t
    x2 = x3.reshape(n, d)
    c_conv = p["w_dw"].shape[1]
    qk_w = n_heads * 2 * (d // n_heads // 4)
    v_w = qk_w
    o_q = 2 * c_conv
    o_k = o_q + qk_w
    o_v = o_k + qk_w
    o_g = o_v + v_w
    tm = _pick(n, 1024)
    tn = 512
    rows = max(t, tm)
    tabs = _rope_tables(pos, rows)
    scale = (d // n_heads // 4) ** -0.5
    g1 = p["norm1_g"]
    w_in = p["w_in"]

    glu = _in_proj("glu", x2, g1, w_in, 0, c_conv, tm, tn, F32, (p["b_glu"],))
    q = _in_proj("rope", x2, g1, w_in, o_q, qk_w, tm, tn, BF16, (tabs, scale))
    k = _in_proj("rope", x2, g1, w_in, o_k, qk_w, tm, tn, F32, (tabs, 1.0))
    v = _in_proj("plain", x2, g1, w_in, o_v, v_w, tm, tn, F32, ("none",))
    gates = _in_proj("plain", x2, g1, w_in, o_g, 2 * d, tm, tn, BF16, ("sigmoid",))

    glu3 = glu.reshape(b, t, c_conv)
    gates3 = gates.reshape(b, t, 2 * d)
    gc = _conv_branch(glu3, hist, p["w_dw"], p["b_dw"], p["ln_g"], p["ln_b"], p["w_co"], p["b_co"],
                      gates3, _pick(t, 256))

    q3 = q.reshape(b, t, qk_w)
    k3 = k.reshape(b, t, qk_w)
    v3 = v.reshape(b, t, v_w)
    if k_past is None:
        o = _attn_prompt(q3, k3, v3, p["lam4"], p["subln_g"], lam_init, n_heads, _pick(t, 256))
    else:
        o = _attn_sample(q3, k_past, v_past, k3, v3, p["lam4"], p["subln_g"], lam_init, n_heads)

    x1, xn2, route = _mid(x2, gc.reshape(n, d), gates, o.reshape(n, v_w), p["w_ao"], p["w_out"],
                          p["norm2_g"], p["wr"], p["br"], p["n_groups"], p["epg"], _pick(n, 256))
    new_hist = glu3[:, t - (p["w_dw"].shape[0] - 1):, :]
    return x1, xn2, route, k3, v3, new_hist


def kernel(x_prompt, x_sample, cache_k, cache_v, state_conv, norm1_g, w_in, b_glu, w_dw, b_dw, conv_ln_g, conv_ln_b, w_conv_out, b_conv_out, lambda_q1, lambda_k1, lambda_q2, lambda_k2, subln_g, w_attn_out, w_out, norm2_g, w_router_group, b_router_group, w_router_expert, b_router_expert, w_exp_gate, w_exp_up, w_exp_down, final_norm_g):
    bp, tp, d = x_prompt.shape
    bs, ts, _ = x_sample.shape
    depth = w_in.shape[0]
    assert depth == 1
    l = 0
    past = cache_k.shape[2]
    n_heads = cache_k.shape[3]
    hd = cache_k.shape[5]
    v_dim = cache_v.shape[4]
    conv_w = w_dw.shape[1]
    c_conv = w_dw.shape[2]
    n_groups = w_router_group.shape[2]
    n_exp = w_router_expert.shape[2]
    epg = n_exp // n_groups
    lam_init = 0.8 - 0.6 * math.exp(-0.3 * l)

    wr = jnp.zeros((d, LANES), F32)
    wr = wr.at[:, :n_exp].set(w_router_expert[l]).at[:, n_exp:n_exp + n_groups].set(w_router_group[l])
    br = jnp.zeros((1, LANES), F32)
    br = br.at[0, :n_exp].set(b_router_expert[l]).at[0, n_exp:n_exp + n_groups].set(b_router_group[l])
    p = dict(
        norm1_g=norm1_g[l][None], w_in=w_in[l].astype(BF16), b_glu=b_glu[l][None],
        w_dw=w_dw[l], b_dw=b_dw[l][None], ln_g=conv_ln_g[l][None], ln_b=conv_ln_b[l][None],
        w_co=w_conv_out[l].astype(BF16), b_co=b_conv_out[l][None],
        lam4=jnp.stack([lambda_q1[l], lambda_k1[l], lambda_q2[l], lambda_k2[l]]),
        subln_g=subln_g[l][None], w_ao=w_attn_out[l].astype(BF16), w_out=w_out[l].astype(BF16),
        norm2_g=norm2_g[l][None], wr=wr, br=br, n_groups=n_groups, epg=epg)

    hist_p = jnp.zeros((bp, HIST_PAD, c_conv), F32)
    hist_s = jnp.concatenate(
        [jnp.zeros((bs, HIST_PAD - (conv_w - 1), c_conv), F32), state_conv[l]], axis=1)
    ck3 = cache_k[l].reshape(bs, past, n_heads * 2 * hd)
    cv3 = cache_v[l].reshape(bs, past, n_heads * v_dim)

    x1p, xn2p, routep, kp, vp, cp = _mixer(x_prompt, hist_p, None, None, jnp.arange(tp), p, n_heads, lam_init)
    x1s, xn2s, routes, ks, vs, cs = _mixer(x_sample, hist_s, ck3, cv3, past + jnp.arange(ts), p, n_heads, lam_init)

    np_, ns_ = bp * tp, bs * ts
    n = np_ + ns_
    tme = 256
    e_idx = jnp.concatenate([routep[:, :2], routes[:, :2]], axis=0).astype(I32)
    onehot = (e_idx[:, :, None] == jnp.arange(n_exp, dtype=I32)[None, None, :]).astype(I32)
    per_tok = onehot.sum(axis=1)
    before = jnp.cumsum(per_tok, axis=0) - per_tok
    rank = jnp.sum(onehot * before[:, None, :], axis=-1)
    counts = per_tok.sum(axis=0)
    tiles_per = (counts + tme - 1) // tme
    tile_end = jnp.cumsum(tiles_per)
    tile_start = tile_end - tiles_per
    slots = (jnp.sum(onehot * (tile_start * tme)[None, None, :], axis=-1) + rank).astype(I32)
    max_tiles = (2 * n) // tme + n_exp
    tile_ids = jnp.arange(max_tiles, dtype=I32)
    tile_expert = jnp.minimum(jnp.sum(tile_ids[:, None] >= tile_end[None, :], axis=1), n_exp - 1).astype(I32)
    n_tiles = tile_end[-1:].astype(I32)

    xs = jnp.zeros((max_tiles * tme, d), F32)
    xs = _dispatch(slots[:np_].reshape(-1), xn2p, xs, _pick(np_, 1024))
    xs = _dispatch(slots[np_:].reshape(-1), xn2s, xs, _pick(ns_, 1024))
    ys = _experts(tile_expert, n_tiles, xs, w_exp_gate[l].astype(BF16), w_exp_up[l].astype(BF16),
                  w_exp_down[l].astype(BF16), tme)
    gf = final_norm_g[None]
    yp = _combine(slots[:np_].reshape(-1), x1p, routep, gf, ys, _pick(np_, 256))
    ys_ = _combine(slots[np_:].reshape(-1), x1s, routes, gf, ys, _pick(ns_, 256))

    return (yp.reshape(bp, tp, d), ys_.reshape(bs, ts, d),
            kp.reshape(1, bp, tp, n_heads, 2, hd), vp.reshape(1, bp, tp, n_heads, v_dim), cp[None],
            ks.reshape(1, bs, ts, n_heads, 2, hd), vs.reshape(1, bs, ts, n_heads, v_dim), cs[None])
```

```python
import functools
import math

import jax
import jax.numpy as jnp
from jax import lax
from jax.experimental import pallas as pl
from jax.experimental.pallas import tpu as pltpu

F32 = jnp.float32
BF16 = jnp.bfloat16
I32 = jnp.int32

EPS = 1e-6
CHUNK = 64
ROPE_DIM = 16
ROPE_THETA = 500000.0
LANES = 128
HIST_PAD = 32
VMEM_LIMIT = 56 * 1024 * 1024


def _cparams(sem):
    return pltpu.CompilerParams(dimension_semantics=sem, vmem_limit_bytes=VMEM_LIMIT)


def _full(shape):
    return pl.BlockSpec(shape, lambda *_: (0,) * len(shape))


def _rms_rows(x_ref, g_ref, xn_ref):
    tm = x_ref.shape[0]
    rc = min(tm, 128)

    def body(r, carry):
        r0 = pl.multiple_of(r * rc, rc)
        xf = x_ref[pl.ds(r0, rc), :]
        ms = jnp.mean(xf * xf, axis=-1, keepdims=True)
        y = xf * lax.rsqrt(ms + EPS) * g_ref[...]
        xn_ref[pl.ds(r0, rc), :] = y.astype(xn_ref.dtype)
        return carry

    lax.fori_loop(0, tm // rc, body, 0)


def _rope(z, c_ref, s1_ref, s2_ref):
    outs = []
    for c in range(z.shape[1] // LANES):
        zc = z[:, c * LANES:(c + 1) * LANES]
        up = pltpu.roll(zc, LANES - ROPE_DIM // 2, axis=1)
        dn = pltpu.roll(zc, ROPE_DIM // 2, axis=1)
        outs.append(zc * c_ref[...] + up * s1_ref[...] + dn * s2_ref[...])
    return outs[0] if len(outs) == 1 else jnp.concatenate(outs, axis=1)


def _in_glu_kernel(x_ref, g_ref, wa_ref, wb_ref, ba_ref, bb_ref, o_ref, xn_ref):
    @pl.when(pl.program_id(1) == 0)
    def _():
        _rms_rows(x_ref, g_ref, xn_ref)

    xn = xn_ref[...]
    a = jnp.dot(xn, wa_ref[...], preferred_element_type=F32) + ba_ref[...]
    b = jnp.dot(xn, wb_ref[...], preferred_element_type=F32) + bb_ref[...]
    o_ref[...] = a * jax.nn.sigmoid(b)


def _in_rope_kernel(scale, x_ref, g_ref, w_ref, c_ref, s1_ref, s2_ref, o_ref, xn_ref):
    @pl.when(pl.program_id(1) == 0)
    def _():
        _rms_rows(x_ref, g_ref, xn_ref)

    z = jnp.dot(xn_ref[...], w_ref[...], preferred_element_type=F32)
    z = _rope(z, c_ref, s1_ref, s2_ref)
    if scale != 1.0:
        z = z * scale
    o_ref[...] = z.astype(o_ref.dtype)


def _in_plain_kernel(act, x_ref, g_ref, w_ref, o_ref, xn_ref):
    @pl.when(pl.program_id(1) == 0)
    def _():
        _rms_rows(x_ref, g_ref, xn_ref)

    z = jnp.dot(xn_ref[...], w_ref[...], preferred_element_type=F32)
    if act == "sigmoid":
        z = jax.nn.sigmoid(z)
    o_ref[...] = z.astype(o_ref.dtype)


def _in_proj(mode, x2, g, w_bf, col0, ncols, tm, tn, out_dtype, extra=()):
    n, d = x2.shape
    grid = (n // tm, ncols // tn)
    cb0 = col0 // tn
    x_spec = pl.BlockSpec((tm, d), lambda i, j: (i, 0))
    g_spec = _full((1, d))
    w_spec = pl.BlockSpec((d, tn), lambda i, j: (0, cb0 + j))
    o_spec = pl.BlockSpec((tm, tn), lambda i, j: (i, j))
    scratch = [pltpu.VMEM((tm, d), BF16)]
    if mode == "glu":
        b_glu, = extra
        half = ncols
        wb_spec = pl.BlockSpec((d, tn), lambda i, j: (0, cb0 + half // tn + j))
        ba_spec = pl.BlockSpec((1, tn), lambda i, j: (0, j))
        bb_spec = pl.BlockSpec((1, tn), lambda i, j: (0, half // tn + j))
        return pl.pallas_call(
            _in_glu_kernel, grid=grid,
            in_specs=[x_spec, g_spec, w_spec, wb_spec, ba_spec, bb_spec],
            out_specs=o_spec, out_shape=jax.ShapeDtypeStruct((n, ncols), out_dtype),
            scratch_shapes=scratch, compiler_params=_cparams(("parallel", "arbitrary")),
            name="in_glu")(x2, g, w_bf, w_bf, b_glu, b_glu)
    if mode == "rope":
        tabs, scale = extra
        nper = tabs[0].shape[0] // tm
        t_spec = pl.BlockSpec((tm, LANES), lambda i, j: (i % nper, 0))
        return pl.pallas_call(
            functools.partial(_in_rope_kernel, scale), grid=grid,
            in_specs=[x_spec, g_spec, w_spec, t_spec, t_spec, t_spec],
            out_specs=o_spec, out_shape=jax.ShapeDtypeStruct((n, ncols), out_dtype),
            scratch_shapes=scratch, compiler_params=_cparams(("parallel", "arbitrary")),
            name="in_rope")(x2, g, w_bf, *tabs)
    act, = extra
    return pl.pallas_call(
        functools.partial(_in_plain_kernel, act), grid=grid,
        in_specs=[x_spec, g_spec, w_spec],
        out_specs=o_spec, out_shape=jax.ShapeDtypeStruct((n, ncols), out_dtype),
        scratch_shapes=scratch, compiler_params=_cparams(("parallel", "arbitrary")),
        name="in_" + act)(x2, g, w_bf)


def _rope_tables(pos, rows):
    half = ROPE_DIM // 2
    inv_freq = 1.0 / (ROPE_THETA ** (jnp.arange(0, ROPE_DIM, 2, dtype=F32) / ROPE_DIM))
    ang = pos.astype(F32)[:, None] * inv_freq[None, :]
    cos, sin = jnp.cos(ang), jnp.sin(ang)
    t = pos.shape[0]
    sub = 64
    ones = jnp.ones((t, sub - ROPE_DIM), F32)
    zeros = jnp.zeros((t, sub - ROPE_DIM), F32)
    zh = jnp.zeros((t, half), F32)
    c = jnp.concatenate([cos, cos, ones], axis=1)
    s1 = jnp.concatenate([-sin, zh, zeros], axis=1)
    s2 = jnp.concatenate([zh, sin, zeros], axis=1)
    reps = (rows // t, LANES // sub)
    return tuple(jnp.tile(a, reps) for a in (c, s1, s2))


def _conv_kernel(conv_w, glu_ref, hist_ref, wdw_ref, bdw_ref, lng_ref, lnb_ref, wco_ref, bco_ref,
                 gate_ref, o_ref, xp_ref, c_ref):
    tt = glu_ref.shape[1]
    i = pl.program_id(1)

    @pl.when(i == 0)
    def _():
        xp_ref[0:HIST_PAD, :] = hist_ref[0]

    @pl.when(i > 0)
    def _():
        xp_ref[0:HIST_PAD, :] = xp_ref[tt:tt + HIST_PAD, :]

    xp_ref[HIST_PAD:HIST_PAD + tt, :] = glu_ref[0]

    rc = 16
    off = HIST_PAD - (conv_w - 1)
    for r in range(tt // rc):
        acc = xp_ref[pl.ds(r * rc + off, rc), :] * wdw_ref[0:1, :]
        for j in range(1, conv_w):
            acc = acc + xp_ref[pl.ds(r * rc + off + j, rc), :] * wdw_ref[j:j + 1, :]
        acc = acc + bdw_ref[...]
        mu = jnp.mean(acc, axis=-1, keepdims=True)
        xc = acc - mu
        var = jnp.mean(xc * xc, axis=-1, keepdims=True)
        y = xc * lax.rsqrt(var + EPS) * lng_ref[...] + lnb_ref[...]
        y = y * jax.nn.sigmoid(y)
        c_ref[r * rc:(r + 1) * rc, :] = y.astype(BF16)

    out = jnp.dot(c_ref[...], wco_ref[...], preferred_element_type=F32) + bco_ref[...]
    o_ref[0] = (gate_ref[0].astype(F32) * out).astype(o_ref.dtype)


def _conv_branch(glu3, hist, w_dw, b_dw, ln_g, ln_b, w_co_bf, b_co, gates3, tt):
    b, t, c = glu3.shape
    d = w_co_bf.shape[1]
    conv_w = w_dw.shape[0]
    return pl.pallas_call(
        functools.partial(_conv_kernel, conv_w), grid=(b, t // tt),
        in_specs=[pl.BlockSpec((1, tt, c), lambda bi, i: (bi, i, 0)),
                  pl.BlockSpec((1, HIST_PAD, c), lambda bi, i: (bi, 0, 0)),
                  _full((conv_w, c)), _full((1, c)), _full((1, c)), _full((1, c)),
                  _full((c, d)), _full((1, d)),
                  pl.BlockSpec((1, tt, d), lambda bi, i: (bi, i, 0))],
        out_specs=pl.BlockSpec((1, tt, d), lambda bi, i: (bi, i, 0)),
        out_shape=jax.ShapeDtypeStruct((b, t, d), BF16),
        scratch_shapes=[pltpu.VMEM((HIST_PAD + tt, c), F32), pltpu.VMEM((tt, c), BF16)],
        compiler_params=_cparams(("parallel", "arbitrary")),
        name="conv_branch")(glu3, hist, w_dw, b_dw, ln_g, ln_b, w_co_bf, b_co, gates3)


def _lambda(lam_ref, lam_init):
    lq1, lk1, lq2, lk2 = (lam_ref[r:r + 1, :] for r in range(4))
    return (jnp.exp(jnp.sum(lq1 * lk1, axis=-1, keepdims=True))
            - jnp.exp(jnp.sum(lq2 * lk2, axis=-1, keepdims=True)) + lam_init)


def _subln(o, g_ref, lam_init):
    y = o * lax.rsqrt(jnp.mean(o * o, axis=-1, keepdims=True) + EPS)
    return y * g_ref[...] * (1.0 - lam_init)


def _qk(q, k):
    return lax.dot_general(q, k, (((1,), (1,)), ((), ())), preferred_element_type=F32)


def _attn_prompt_kernel(lam_init, q_ref, k_ref, v_ref, lam_ref, sg_ref, o_ref, kb_ref, vb_ref):
    tq = q_ref.shape[1]
    hd = q_ref.shape[2] // 2
    qi = pl.program_id(2)

    @pl.when(qi == 0)
    def _():
        kb_ref[...] = k_ref[0].astype(BF16)
        vb_ref[...] = v_ref[0].astype(BF16)

    q = q_ref[0]
    qs = (q[:, :hd], q[:, hd:])

    def step(k0, state, masked):
        kblk = kb_ref[pl.ds(k0, tq), :]
        vblk = vb_ref[pl.ds(k0, tq), :]
        new = []
        for m in range(2):
            mx, l, acc = state[m]
            s = _qk(qs[m], kblk[:, m * hd:(m + 1) * hd])
            if masked:
                qc = lax.broadcasted_iota(I32, (tq, tq), 0) // CHUNK
                kc = lax.broadcasted_iota(I32, (tq, tq), 1) // CHUNK
                s = jnp.where(kc <= qc, s, -jnp.inf)
            mn = jnp.maximum(mx, jnp.max(s, axis=-1, keepdims=True))
            alpha = jnp.exp(mx - mn)
            p = jnp.exp(s - mn)
            l = alpha * l + jnp.sum(p, axis=-1, keepdims=True)
            acc = alpha * acc + jnp.dot(p.astype(BF16), vblk, preferred_element_type=F32)
            new.append((mn, l, acc))
        return tuple(new)

    init = tuple((jnp.full((tq, 1), -jnp.inf, F32), jnp.zeros((tq, 1), F32),
                  jnp.zeros((tq, v_ref.shape[2]), F32)) for _ in range(2))
    state = lax.fori_loop(0, qi, lambda kb, st: step(pl.multiple_of(kb * tq, tq), st, False), init)
    state = step(pl.multiple_of(qi * tq, tq), state, True)
    lam = _lambda(lam_ref, lam_init)
    o = state[0][2] / state[0][1] - lam * (state[1][2] / state[1][1])
    o_ref[0] = _subln(o, sg_ref, lam_init).astype(o_ref.dtype)


def _attn_prompt(q3, k3, v3, lam4, subln_g, lam_init, n_heads, tq):
    b, t, w = q3.shape
    hw = w // n_heads
    blk = lambda bi, h, i: (bi, i, h)
    whole = lambda bi, h, i: (bi, 0, h)
    return pl.pallas_call(
        functools.partial(_attn_prompt_kernel, lam_init), grid=(b, n_heads, t // tq),
        in_specs=[pl.BlockSpec((1, tq, hw), blk), pl.BlockSpec((1, t, hw), whole),
                  pl.BlockSpec((1, t, hw), whole), _full(lam4.shape), _full(subln_g.shape)],
        out_specs=pl.BlockSpec((1, tq, hw), blk),
        out_shape=jax.ShapeDtypeStruct((b, t, w), BF16),
        scratch_shapes=[pltpu.VMEM((t, hw), BF16), pltpu.VMEM((t, hw), BF16)],
        compiler_params=_cparams(("parallel", "parallel", "arbitrary")),
        name="attn_prompt")(q3, k3, v3, lam4, subln_g)


def _attn_sample_kernel(lam_init, q_ref, ck_ref, cv_ref, kn_ref, vn_ref, lam_ref, sg_ref, o_ref):
    hd = q_ref.shape[2] // 2
    q = q_ref[0]
    ck = ck_ref[0].astype(BF16)
    cv = cv_ref[0].astype(BF16)
    kn = kn_ref[0].astype(BF16)
    vn = vn_ref[0].astype(BF16)
    outs = []
    for m in range(2):
        qm = q[:, m * hd:(m + 1) * hd]
        sp = _qk(qm, ck[:, m * hd:(m + 1) * hd])
        sn = _qk(qm, kn[:, m * hd:(m + 1) * hd])
        mx = jnp.maximum(jnp.max(sp, axis=-1, keepdims=True), jnp.max(sn, axis=-1, keepdims=True))
        pp = jnp.exp(sp - mx)
        pn = jnp.exp(sn - mx)
        l = jnp.sum(pp, axis=-1, keepdims=True) + jnp.sum(pn, axis=-1, keepdims=True)
        acc = (jnp.dot(pp.astype(BF16), cv, preferred_element_type=F32)
               + jnp.dot(pn.astype(BF16), vn, preferred_element_type=F32))
        outs.append(acc / l)
    lam = _lambda(lam_ref, lam_init)
    o = outs[0] - lam * outs[1]
    o_ref[0] = _subln(o, sg_ref, lam_init).astype(o_ref.dtype)


def _attn_sample(q3, ck3, cv3, kn3, vn3, lam4, subln_g, lam_init, n_heads):
    b, t, w = q3.shape
    past = ck3.shape[1]
    hw = w // n_heads
    idx = lambda bi, h: (bi, 0, h)
    return pl.pallas_call(
        functools.partial(_attn_sample_kernel, lam_init), grid=(b, n_heads),
        in_specs=[pl.BlockSpec((1, t, hw), idx), pl.BlockSpec((1, past, hw), idx),
                  pl.BlockSpec((1, past, hw), idx), pl.BlockSpec((1, t, hw), idx),
                  pl.BlockSpec((1, t, hw), idx), _full(lam4.shape), _full(subln_g.shape)],
        out_specs=pl.BlockSpec((1, t, hw), idx),
        out_shape=jax.ShapeDtypeStruct((b, t, w), BF16),
        compiler_params=_cparams(("parallel", "parallel")),
        name="attn_sample")(q3, ck3, cv3, kn3, vn3, lam4, subln_g)


def _mid_kernel(n_groups, epg, x_ref, gc_ref, ga_ref, o_ref, wao_ref, wout_ref, g2_ref, wr_ref, br_ref,
                x1_ref, xn2_ref, route_ref):
    attn = jnp.dot(o_ref[...], wao_ref[...], preferred_element_type=F32)
    merged = gc_ref[...].astype(F32) + ga_ref[...].astype(F32) * attn
    x1 = x_ref[...] + jnp.dot(merged.astype(BF16), wout_ref[...], preferred_element_type=F32)
    x1_ref[...] = x1
    xn2 = x1 * lax.rsqrt(jnp.mean(x1 * x1, axis=-1, keepdims=True) + EPS) * g2_ref[...]
    xn2_ref[...] = xn2

    n_exp = n_groups * epg
    lg = jnp.dot(xn2, wr_ref[...], preferred_element_type=F32, precision=lax.Precision.HIGHEST) + br_ref[...]
    lane = lax.broadcasted_iota(I32, lg.shape, 1)
    neg = -jnp.inf
    big = jnp.int32(2 ** 30)
    is_g = (lane >= n_exp) & (lane < n_exp + n_groups)
    gl = jnp.where(is_g, lg, neg)
    gmax = jnp.max(gl, axis=-1, keepdims=True)
    g_idx = jnp.min(jnp.where(gl == gmax, lane, big), axis=-1, keepdims=True) - n_exp
    p_g = 1.0 / jnp.sum(jnp.where(is_g, jnp.exp(gl - gmax), 0.0), axis=-1, keepdims=True)
    in_grp = (lane >= g_idx * epg) & (lane < (g_idx + 1) * epg)
    el = jnp.where(in_grp, lg, neg)
    v1 = jnp.max(el, axis=-1, keepdims=True)
    e1 = jnp.min(jnp.where(el == v1, lane, big), axis=-1, keepdims=True)
    el2 = jnp.where(lane == e1, neg, el)
    v2 = jnp.max(el2, axis=-1, keepdims=True)
    e2 = jnp.min(jnp.where(el2 == v2, lane, big), axis=-1, keepdims=True)
    t2 = jnp.exp(v2 - v1)
    den = 1.0 + t2
    w1 = p_g / den
    w2 = p_g * t2 / den
    route = jnp.where(lane == 0, e1.astype(F32),
                      jnp.where(lane == 1, e2.astype(F32),
                                jnp.where(lane == 2, w1, jnp.where(lane == 3, w2, 0.0))))
    route_ref[...] = route


def _mid(x2, gc2, gates2, o2, w_ao_bf, w_out_bf, g2, wr, br, n_groups, epg, tm):
    n, d = x2.shape
    wo = o2.shape[1]
    row = lambda i: (i, 0)
    return pl.pallas_call(
        functools.partial(_mid_kernel, n_groups, epg), grid=(n // tm,),
        in_specs=[pl.BlockSpec((tm, d), row), pl.BlockSpec((tm, d), row),
                  pl.BlockSpec((tm, d), lambda i: (i, 1)), pl.BlockSpec((tm, wo), row),
                  _full(w_ao_bf.shape), _full(w_out_bf.shape), _full((1, d)), _full(wr.shape), _full(br.shape)],
        out_specs=[pl.BlockSpec((tm, d), row), pl.BlockSpec((tm, d), row), pl.BlockSpec((tm, LANES), row)],
        out_shape=[jax.ShapeDtypeStruct((n, d), F32), jax.ShapeDtypeStruct((n, d), F32),
                   jax.ShapeDtypeStruct((n, LANES), F32)],
        compiler_params=_cparams(("parallel",)),
        name="mid")(x2, gc2, gates2, o2, w_ao_bf, w_out_bf, g2, wr, br)


def _dispatch_kernel(slot_ref, xn_ref, xs_in_hbm, xs_hbm, sem):
    del xs_in_hbm
    tmd = xn_ref.shape[0]

    def row_copy(r, s):
        return pltpu.make_async_copy(xn_ref.at[pl.ds(r, 1)], xs_hbm.at[pl.ds(s, 1)], sem)

    def issue(r, carry):
        row_copy(r, slot_ref[0, 0, 2 * r]).start()
        row_copy(r, slot_ref[0, 0, 2 * r + 1]).start()
        return carry

    def drain(r, carry):
        row_copy(0, 0).wait()
        row_copy(0, 0).wait()
        return carry

    lax.fori_loop(0, tmd, issue, 0)
    lax.fori_loop(0, tmd, drain, 0)


def _dispatch(slots, xn2, xs, tmd):
    n, d = xn2.shape
    nt = n // tmd
    return pl.pallas_call(
        _dispatch_kernel, grid=(nt,),
        in_specs=[pl.BlockSpec((1, 1, 2 * tmd), lambda i: (i, 0, 0), memory_space=pltpu.SMEM),
                  pl.BlockSpec((tmd, d), lambda i: (i, 0)), pl.BlockSpec(memory_space=pl.ANY)],
        out_specs=pl.BlockSpec(memory_space=pl.ANY),
        out_shape=jax.ShapeDtypeStruct(xs.shape, xs.dtype),
        scratch_shapes=[pltpu.SemaphoreType.DMA],
        input_output_aliases={2: 0},
        compiler_params=_cparams(("arbitrary",)),
        name="dispatch")(slots.reshape(nt, 1, 2 * tmd), xn2, xs)


def _expert_kernel(te_ref, nt_ref, xs_ref, wg_ref, wu_ref, wd_ref, ys_ref):
    del te_ref
    i = pl.program_id(0)

    @pl.when(i < nt_ref[0])
    def _():
        x = xs_ref[...].astype(BF16)
        hg = jnp.dot(x, wg_ref[0], preferred_element_type=F32)
        hu = jnp.dot(x, wu_ref[0], preferred_element_type=F32)
        h = (hg * jax.nn.sigmoid(hg)) * hu
        ys_ref[...] = jnp.dot(h.astype(BF16), wd_ref[0], preferred_element_type=F32)

    @pl.when(i >= nt_ref[0])
    def _():
        ys_ref[...] = jnp.zeros_like(ys_ref)


def _experts(tile_expert, n_tiles, xs, wg_bf, wu_bf, wd_bf, tme):
    s, d = xs.shape
    de = wg_bf.shape[2]
    grid_spec = pltpu.PrefetchScalarGridSpec(
        num_scalar_prefetch=2, grid=(s // tme,),
        in_specs=[pl.BlockSpec((tme, d), lambda i, te, nt: (i, 0)),
                  pl.BlockSpec((1, d, de), lambda i, te, nt: (te[i], 0, 0)),
                  pl.BlockSpec((1, d, de), lambda i, te, nt: (te[i], 0, 0)),
                  pl.BlockSpec((1, de, d), lambda i, te, nt: (te[i], 0, 0))],
        out_specs=pl.BlockSpec((tme, d), lambda i, te, nt: (i, 0)))
    return pl.pallas_call(
        _expert_kernel, grid_spec=grid_spec,
        out_shape=jax.ShapeDtypeStruct((s, d), F32),
        compiler_params=_cparams(("arbitrary",)),
        name="experts")(tile_expert, n_tiles, xs, wg_bf, wu_bf, wd_bf)


def _combine_kernel(slot_ref, x1_ref, route_ref, gf_ref, ys_hbm, y_ref, buf_ref, sem):
    tmf = x1_ref.shape[0]

    def row_copy(s, k, r):
        return pltpu.make_async_copy(ys_hbm.at[pl.ds(s, 1)], buf_ref.at[k, pl.ds(r, 1)], sem)

    def issue(r, carry):
        row_copy(slot_ref[0, 0, 2 * r], 0, r).start()
        row_copy(slot_ref[0, 0, 2 * r + 1], 1, r).start()
        return carry

    def drain(r, carry):
        row_copy(0, 0, 0).wait()
        row_copy(0, 0, 0).wait()
        return carry

    lax.fori_loop(0, tmf, issue, 0)
    lax.fori_loop(0, tmf, drain, 0)
    w1 = route_ref[:, 2:3]
    w2 = route_ref[:, 3:4]
    x2 = x1_ref[...] + (w1 * buf_ref[0] + w2 * buf_ref[1])
    y = x2 * lax.rsqrt(jnp.mean(x2 * x2, axis=-1, keepdims=True) + EPS) * gf_ref[...]
    y_ref[...] = y


def _combine(slots, x1, route, gf, ys, tmf):
    n, d = x1.shape
    nt = n // tmf
    row = lambda i: (i, 0)
    return pl.pallas_call(
        _combine_kernel, grid=(nt,),
        in_specs=[pl.BlockSpec((1, 1, 2 * tmf), lambda i: (i, 0, 0), memory_space=pltpu.SMEM),
                  pl.BlockSpec((tmf, d), row), pl.BlockSpec((tmf, LANES), row), _full((1, d)),
                  pl.BlockSpec(memory_space=pl.ANY)],
        out_specs=pl.BlockSpec((tmf, d), row),
        out_shape=jax.ShapeDtypeStruct((n, d), F32),
        scratch_shapes=[pltpu.VMEM((2, tmf, d), F32), pltpu.SemaphoreType.DMA],
        compiler_params=_cparams(("arbitrary",)),
        name="combine")(slots.reshape(nt, 1, 2 * tmf), x1, route, gf, ys)


def _pick(n, pref):
    t = min(n, pref)
    while n % t:
        t //= 2
    return t


def _mixer(x3, hist, k_past, v_past, pos, p, n_heads, lam_init):
    b, t, d = x3.shape
    n = b * t
    x2 = x3.reshape(n, d)
    c_conv = p["w_dw"].shape[1]
    qk_w = n_heads * 2 * (d // n_heads // 4)
    v_w = qk_w
    o_q = 2 * c_conv
    o_k = o_q + qk_w
    o_v = o_k + qk_w
    o_g = o_v + v_w
    tm = _pick(n, 1024)
    tn = 512
    rows = max(t, tm)
    tabs = _rope_tables(pos, rows)
    scale = (d // n_heads // 4) ** -0.5
    g1 = p["norm1_g"]
    w_in = p["w_in"]

    glu = _in_proj("glu", x2, g1, w_in, 0, c_conv, tm, tn, F32, (p["b_glu"],))
    q = _in_proj("rope", x2, g1, w_in, o_q, qk_w, tm, tn, BF16, (tabs, scale))
    k = _in_proj("rope", x2, g1, w_in, o_k, qk_w, tm, tn, F32, (tabs, 1.0))
    v = _in_proj("plain", x2, g1, w_in, o_v, v_w, tm, tn, F32, ("none",))
    gates = _in_proj("plain", x2, g1, w_in, o_g, 2 * d, tm, tn, BF16, ("sigmoid",))

    glu3 = glu.reshape(b, t, c_conv)
    gates3 = gates.reshape(b, t, 2 * d)
    gc = _conv_branch(glu3, hist, p["w_dw"], p["b_dw"], p["ln_g"], p["ln_b"], p["w_co"], p["b_co"],
                      gates3, _pick(t, 256))

    q3 = q.reshape(b, t, qk_w)
    k3 = k.reshape(b, t, qk_w)
    v3 = v.reshape(b, t, v_w)
    if k_past is None:
        o = _attn_prompt(q3, k3, v3, p["lam4"], p["subln_g"], lam_init, n_heads, _pick(t, 256))
    else:
        o = _attn_sample(q3, k_past, v_past, k3, v3, p["lam4"], p["subln_g"], lam_init, n_heads)

    x1, xn2, route = _mid(x2, gc.reshape(n, d), gates, o.reshape(n, v_w), p["w_ao"], p["w_out"],
                          p["norm2_g"], p["wr"], p["br"], p["n_groups"], p["epg"], _pick(n, 256))
    new_hist = glu3[:, t - (p["w_dw"].shape[0] - 1):, :]
    return x1, xn2, route, k3, v3, new_hist


def kernel(x_prompt, x_sample, cache_k, cache_v, state_conv, norm1_g, w_in, b_glu, w_dw, b_dw, conv_ln_g, conv_ln_b, w_conv_out, b_conv_out, lambda_q1, lambda_k1, lambda_q2, lambda_k2, subln_g, w_attn_out, w_out, norm2_g, w_router_group, b_router_group, w_router_expert, b_router_expert, w_exp_gate, w_exp_up, w_exp_down, final_norm_g):
    bp, tp, d = x_prompt.shape
    bs, ts, _ = x_sample.shape
    depth = w_in.shape[0]
    assert depth == 1
    l = 0
    past = cache_k.shape[2]
    n_heads = cache_k.shape[3]
    hd = cache_k.shape[5]
    v_dim = cache_v.shape[4]
    conv_w = w_dw.shape[1]
    c_conv = w_dw.shape[2]
    n_groups = w_router_group.shape[2]
    n_exp = w_router_expert.shape[2]
    epg = n_exp // n_groups
    lam_init = 0.8 - 0.6 * math.exp(-0.3 * l)

    wr = jnp.zeros((d, LANES), F32)
    wr = wr.at[:, :n_exp].set(w_router_expert[l]).at[:, n_exp:n_exp + n_groups].set(w_router_group[l])
    br = jnp.zeros((1, LANES), F32)
    br = br.at[0, :n_exp].set(b_router_expert[l]).at[0, n_exp:n_exp + n_groups].set(b_router_group[l])
    p = dict(
        norm1_g=norm1_g[l][None], w_in=w_in[l].astype(BF16), b_glu=b_glu[l][None],
        w_dw=w_dw[l], b_dw=b_dw[l][None], ln_g=conv_ln_g[l][None], ln_b=conv_ln_b[l][None],
        w_co=w_conv_out[l].astype(BF16), b_co=b_conv_out[l][None],
        lam4=jnp.stack([lambda_q1[l], lambda_k1[l], lambda_q2[l], lambda_k2[l]]),
        subln_g=subln_g[l][None], w_ao=w_attn_out[l].astype(BF16), w_out=w_out[l].astype(BF16),
        norm2_g=norm2_g[l][None], wr=wr, br=br, n_groups=n_groups, epg=epg)

    hist_p = jnp.zeros((bp, HIST_PAD, c_conv), F32)
    hist_s = jnp.concatenate(
        [jnp.zeros((bs, HIST_PAD - (conv_w - 1), c_conv), F32), state_conv[l]], axis=1)
    ck3 = cache_k[l].reshape(bs, past, n_heads * 2 * hd)
    cv3 = cache_v[l].reshape(bs, past, n_heads * v_dim)

    x1p, xn2p, routep, kp, vp, cp = _mixer(x_prompt, hist_p, None, None, jnp.arange(tp), p, n_heads, lam_init)
    x1s, xn2s, routes, ks, vs, cs = _mixer(x_sample, hist_s, ck3, cv3, past + jnp.arange(ts), p, n_heads, lam_init)

    np_, ns_ = bp * tp, bs * ts
    n = np_ + ns_
    tme = 256
    e_idx = jnp.concatenate([routep[:, :2], routes[:, :2]], axis=0).astype(I32)
    onehot = (e_idx[:, :, None] == jnp.arange(n_exp, dtype=I32)[None, None, :]).astype(I32)
    per_tok = onehot.sum(axis=1)
    before = jnp.cumsum(per_tok, axis=0) - per_tok
    rank = jnp.sum(onehot * before[:, None, :], axis=-1)
    counts = per_tok.sum(axis=0)
    tiles_per = (counts + tme - 1) // tme
    tile_end = jnp.cumsum(tiles_per)
    tile_start = tile_end - tiles_per
    slots = (jnp.sum(onehot * (tile_start * tme)[None, None, :], axis=-1) + rank).astype(I32)
    max_tiles = (2 * n) // tme + n_exp
    tile_ids = jnp.arange(max_tiles, dtype=I32)
    tile_expert = jnp.minimum(jnp.sum(tile_ids[:, None] >= tile_end[None, :], axis=1), n_exp - 1).astype(I32)
    n_tiles = tile_end[-1:].astype(I32)

    xs = jnp.zeros((max_tiles * tme, d), F32)
    xs = _dispatch(slots[:np_].reshape(-1), xn2p, xs, _pick(np_, 256))
    xs = _dispatch(slots[np_:].reshape(-1), xn2s, xs, _pick(ns_, 256))
    ys = _experts(tile_expert, n_tiles, xs, w_exp_gate[l].astype(BF16), w_exp_up[l].astype(BF16),
                  w_exp_down[l].astype(BF16), tme)
    gf = final_norm_g[None]
    yp = _combine(slots[:np_].reshape(-1), x1p, routep, gf, ys, _pick(np_, 256))
    ys_ = _combine(slots[np_:].reshape(-1), x1s, routes, gf, ys, _pick(ns_, 256))

    return (yp.reshape(bp, tp, d), ys_.reshape(bs, ts, d),
            kp.reshape(1, bp, tp, n_heads, 2, hd), vp.reshape(1, bp, tp, n_heads, v_dim), cp[None],
            ks.reshape(1, bs, ts, n_heads, 2, hd), vs.reshape(1, bs, ts, n_heads, v_dim), cs[None])
```

```python
import functools
import math

import jax
import jax.numpy as jnp
from jax import lax
from jax.experimental import pallas as pl
from jax.experimental.pallas import tpu as pltpu

F32 = jnp.float32
BF16 = jnp.bfloat16
I32 = jnp.int32

EPS = 1e-6
CHUNK = 64
ROPE_DIM = 16
ROPE_THETA = 500000.0
LOG2E = 1.4426950408889634
LANES = 128
SUBLANES = 8
KV_BLOCK = 256
HIST_PAD = 32
VMEM_LIMIT = 56 * 1024 * 1024


def _cparams(sem):
    return pltpu.CompilerParams(dimension_semantics=sem, vmem_limit_bytes=VMEM_LIMIT)


def _full(shape, single=False):
    idx = lambda *_: (0,) * len(shape)
    if single:
        return pl.BlockSpec(shape, idx, pipeline_mode=pl.Buffered(1))
    return pl.BlockSpec(shape, idx)


def _rms(x, g):
    return x * lax.rsqrt(jnp.mean(x * x, axis=-1, keepdims=True) + EPS) * g


def _rms_rows(x_ref, g_ref, xn_ref):
    tm = x_ref.shape[0]
    rc = min(tm, 128)

    def body(r, carry):
        r0 = pl.multiple_of(r * rc, rc)
        xn_ref[pl.ds(r0, rc), :] = _rms(x_ref[pl.ds(r0, rc), :], g_ref[...]).astype(xn_ref.dtype)
        return carry

    lax.fori_loop(0, tm // rc, body, 0)


def _rope(z, c_ref, s1_ref, s2_ref):
    outs = []
    for c in range(z.shape[1] // LANES):
        zc = z[:, c * LANES:(c + 1) * LANES]
        up = pltpu.roll(zc, LANES - ROPE_DIM // 2, axis=1)
        dn = pltpu.roll(zc, ROPE_DIM // 2, axis=1)
        outs.append(zc * c_ref[...] + up * s1_ref[...] + dn * s2_ref[...])
    return jnp.concatenate(outs, axis=1)


def _in_kernel(prompt, q_scale, x_ref, g_ref, w_ref, b_ref, c_ref, s1_ref, s2_ref, *rest):
    if prompt:
        glu_ref, q_ref, k_ref, kt_ref, v_ref, vt_ref, gate_ref, xn_ref = rest
    else:
        glu_ref, q_ref, k_ref, v_ref, gate_ref, xn_ref = rest
    j = pl.program_id(1)
    tm, tn = x_ref.shape[0], w_ref.shape[1]
    cw = 2 * LANES
    nchunk = tn // cw

    @pl.when(j == 0)
    def _():
        _rms_rows(x_ref, g_ref, xn_ref)

    def zchunk(c0):
        return jnp.dot(xn_ref[...], w_ref[:, c0:c0 + cw], preferred_element_type=F32)

    @pl.when(j < 2)
    def _():
        half = tn // 2
        for c in range(half // cw):
            a = zchunk(c * cw) + b_ref[:, c * cw:(c + 1) * cw]
            b = zchunk(half + c * cw) + b_ref[:, half + c * cw:half + (c + 1) * cw]
            glu_ref[:, c * cw:(c + 1) * cw] = a * jax.nn.sigmoid(b)

    @pl.when(j == 2)
    def _():
        for c in range(nchunk):
            zr = _rope(zchunk(c * cw), c_ref, s1_ref, s2_ref)
            q_ref[:, c * cw:(c + 1) * cw] = (zr * q_scale).astype(q_ref.dtype)

    @pl.when(j == 3)
    def _():
        for c in range(nchunk):
            zr = _rope(zchunk(c * cw), c_ref, s1_ref, s2_ref)
            k_ref[:, c * cw:(c + 1) * cw] = zr.astype(k_ref.dtype)
            if prompt:
                kt_ref[0, c * cw:(c + 1) * cw, :] = zr.T

    @pl.when(j == 4)
    def _():
        for c in range(nchunk):
            z = zchunk(c * cw)
            v_ref[:, c * cw:(c + 1) * cw] = z
            if prompt:
                kb = vt_ref.shape[4]
                for hh in range(cw // LANES):
                    for blk in range(tm // kb):
                        vt_ref[0, c * (cw // LANES) + hh, blk] = (
                            z[blk * kb:(blk + 1) * kb, hh * LANES:(hh + 1) * LANES].T.astype(BF16))

    @pl.when(j >= 5)
    def _():
        for c in range(nchunk):
            gate_ref[:, c * cw:(c + 1) * cw] = jax.nn.sigmoid(zchunk(c * cw)).astype(gate_ref.dtype)


def _in_proj(x2, g, w_cat, b_cat, tabs, q_scale, tm, seq, prompt):
    n, d = x2.shape
    tn = 1024
    nj = w_cat.shape[1] // tn
    ngate = nj - 5
    nper = tabs[0].shape[0] // tm
    row0 = lambda i, j: (i, 0)
    in_specs = [pl.BlockSpec((tm, d), row0), _full((1, d)),
                pl.BlockSpec((d, tn), lambda i, j: (0, j)),
                pl.BlockSpec((1, tn), lambda i, j: (0, jnp.minimum(j, 1)))]
    in_specs += [pl.BlockSpec((tm, LANES), lambda i, j: (i % nper, 0))] * 3
    glu_spec = pl.BlockSpec((tm, tn // 2), lambda i, j: (i, jnp.minimum(j, 1)))
    row_spec = pl.BlockSpec((tm, tn), row0)
    gate_spec = pl.BlockSpec((tm, tn), lambda i, j: (i, jnp.clip(j - 5, 0, ngate - 1)))
    sds = jax.ShapeDtypeStruct
    if prompt:
        tpb = seq // tm
        nb = n // seq
        out_specs = [glu_spec, row_spec, row_spec,
                     pl.BlockSpec((1, tn, tm), lambda i, j: (i // tpb, 0, i % tpb)),
                     row_spec,
                     pl.BlockSpec((1, tn // LANES, tm // KV_BLOCK, LANES, KV_BLOCK),
                                  lambda i, j: (i // tpb, 0, i % tpb, 0, 0)),
                     gate_spec]
        out_shape = [sds((n, tn), F32), sds((n, tn), BF16), sds((n, tn), BF16), sds((nb, tn, seq), F32),
                     sds((n, tn), F32), sds((nb, tn // LANES, seq // KV_BLOCK, LANES, KV_BLOCK), BF16),
                     sds((n, ngate * tn), BF16)]
    else:
        out_specs = [glu_spec, row_spec, row_spec, row_spec, gate_spec]
        out_shape = [sds((n, tn), F32), sds((n, tn), BF16), sds((n, tn), F32), sds((n, tn), F32),
                     sds((n, ngate * tn), BF16)]
    return pl.pallas_call(
        functools.partial(_in_kernel, prompt, q_scale), grid=(n // tm, nj),
        in_specs=in_specs, out_specs=out_specs, out_shape=out_shape,
        scratch_shapes=[pltpu.VMEM((tm, d), BF16)],
        compiler_params=_cparams(("parallel", "arbitrary")),
        name="in_proj")(x2, g, w_cat, b_cat, *tabs)


def _rope_tables(pos, rows):
    half = ROPE_DIM // 2
    inv_freq = 1.0 / (ROPE_THETA ** (jnp.arange(0, ROPE_DIM, 2, dtype=F32) / ROPE_DIM))
    ang = pos.astype(F32)[:, None] * inv_freq[None, :]
    cos, sin = jnp.cos(ang), jnp.sin(ang)
    t = pos.shape[0]
    sub = 64
    ones = jnp.ones((t, sub - ROPE_DIM), F32)
    zeros = jnp.zeros((t, sub - ROPE_DIM), F32)
    zh = jnp.zeros((t, half), F32)
    c = jnp.concatenate([cos, cos, ones], axis=1)
    s1 = jnp.concatenate([-sin, zh, zeros], axis=1)
    s2 = jnp.concatenate([zh, sin, zeros], axis=1)
    reps = (rows // t, LANES // sub)
    return tuple(jnp.tile(a, reps) for a in (c, s1, s2))


def _conv_kernel(conv_w, glu_ref, hist_ref, wdw_ref, bdw_ref, lng_ref, lnb_ref, wco_ref, bco_ref,
                 gate_ref, o_ref, xs_ref, c_ref):
    tt = glu_ref.shape[1]
    rows = HIST_PAD + tt
    i = pl.program_id(1)

    @pl.when(i == 0)
    def _():
        xs_ref[0, 0:HIST_PAD, :] = hist_ref[0]

    @pl.when(i > 0)
    def _():
        xs_ref[0, 0:HIST_PAD, :] = xs_ref[0, tt:tt + HIST_PAD, :]

    xs_ref[0, HIST_PAD:rows, :] = glu_ref[0]
    for b in range(1, SUBLANES):
        xs_ref[b, 0:rows - SUBLANES, :] = xs_ref[0, b:b + rows - SUBLANES, :]

    rc = 16
    off = HIST_PAD - (conv_w - 1)

    def chunk(r, carry):
        r0 = pl.multiple_of(r * rc, rc)
        acc = None
        for j in range(conv_w):
            a, b = divmod(off + j, SUBLANES)
            term = xs_ref[b, pl.ds(pl.multiple_of(r0 + a * SUBLANES, SUBLANES), rc), :] * wdw_ref[j:j + 1, :]
            acc = term if acc is None else acc + term
        acc = acc + bdw_ref[...]
        mu = jnp.mean(acc, axis=-1, keepdims=True)
        xc = acc - mu
        var = jnp.mean(xc * xc, axis=-1, keepdims=True)
        y = xc * lax.rsqrt(var + EPS) * lng_ref[...] + lnb_ref[...]
        y = y * jax.nn.sigmoid(y)
        c_ref[pl.ds(r0, rc), :] = y.astype(BF16)
        return carry

    lax.fori_loop(0, tt // rc, chunk, 0)
    out = jnp.dot(c_ref[...], wco_ref[...], preferred_element_type=F32) + bco_ref[...]
    o_ref[0] = (gate_ref[0].astype(F32) * out).astype(o_ref.dtype)


def _conv_branch(glu3, hist, w_dw, b_dw, ln_g, ln_b, w_co_bf, b_co, gates3, tt):
    b, t, c = glu3.shape
    d = w_co_bf.shape[1]
    conv_w = w_dw.shape[0]
    return pl.pallas_call(
        functools.partial(_conv_kernel, conv_w), grid=(b, t // tt),
        in_specs=[pl.BlockSpec((1, tt, c), lambda bi, i: (bi, i, 0)),
                  pl.BlockSpec((1, HIST_PAD, c), lambda bi, i: (bi, 0, 0)),
                  _full((conv_w, c)), _full((1, c)), _full((1, c)), _full((1, c)),
                  _full((c, d)), _full((1, d)),
                  pl.BlockSpec((1, tt, d), lambda bi, i: (bi, i, 0))],
        out_specs=pl.BlockSpec((1, tt, d), lambda bi, i: (bi, i, 0)),
        out_shape=jax.ShapeDtypeStruct((b, t, d), BF16),
        scratch_shapes=[pltpu.VMEM((SUBLANES, HIST_PAD + tt, c), F32), pltpu.VMEM((tt, c), BF16)],
        compiler_params=_cparams(("parallel", "arbitrary")),
        name="conv_branch")(glu3, hist, w_dw, b_dw, ln_g, ln_b, w_co_bf, b_co, gates3)


def _lambda(lam_ref, lam_init):
    lq1, lk1, lq2, lk2 = (lam_ref[r:r + 1, :] for r in range(4))
    return (jnp.exp(jnp.sum(lq1 * lk1, axis=-1, keepdims=True))
            - jnp.exp(jnp.sum(lq2 * lk2, axis=-1, keepdims=True)) + lam_init)


def _split_maps(q):
    lane = lax.broadcasted_iota(I32, q.shape, 1)
    zero = jnp.zeros_like(q)
    half = q.shape[1] // 2
    return jnp.where(lane < half, q, zero), jnp.where(lane >= half, q, zero)


def _nt(a, b):
    return lax.dot_general(a, b, (((1,), (1,)), ((), ())), preferred_element_type=F32)


def _attn_prompt_kernel(lam_init, q_ref, k_ref, vt_ref, lam_ref, sgt_ref, o_ref, acc_ref, sa_ref, sb_ref):
    tq = q_ref.shape[1]
    tk = vt_ref.shape[4]
    assert tq == tk
    last_blk = k_ref.shape[1] // tk - 1
    qi = pl.program_id(2)
    qz = _split_maps(q_ref[0])
    acc_ref[...] = jnp.zeros_like(acc_ref)
    chunks = tk // CHUNK
    cdiff = (lax.broadcasted_iota(I32, (tk, tq), 0) // CHUNK) - (lax.broadcasted_iota(I32, (tk, tq), 1) // CHUNK)

    def scores(kb, s_ref):
        k0 = pl.multiple_of(jnp.minimum(kb, last_blk) * tk, tk)
        kblk = k_ref[0, pl.ds(k0, tk), :]
        for m in range(2):
            s_ref[m] = _nt(kblk, qz[m])

    def consume(kb, s_ref, carry):
        vis = cdiff <= (qi - kb) * chunks
        vblk = vt_ref[0, 0, jnp.minimum(kb, last_blk)]
        new, alphas, ps = [], [], []
        for m in range(2):
            mx, l = carry[m]
            sm = jnp.where(vis, s_ref[m], -jnp.inf)
            mn = jnp.maximum(mx, jnp.max(sm, axis=0, keepdims=True))
            alpha = jnp.exp2(mx - mn)
            p = jnp.exp2(sm - mn)
            new.append((mn, alpha * l + jnp.sum(p, axis=0, keepdims=True)))
            alphas.append(alpha)
            ps.append(p.astype(BF16))
        pv = [jnp.dot(vblk, ps[m], preferred_element_type=F32) for m in range(2)]
        for m in range(2):
            acc_ref[m] = alphas[m] * acc_ref[m] + pv[m]
        return tuple(new)

    def pair(j, carry):
        scores(2 * j + 1, sb_ref)
        carry = consume(2 * j, sa_ref, carry)
        scores(2 * j + 2, sa_ref)
        return consume(2 * j + 1, sb_ref, carry)

    init = tuple((jnp.full((1, tq), -jnp.inf, F32), jnp.zeros((1, tq), F32)) for _ in range(2))
    scores(0, sa_ref)
    carry = lax.fori_loop(0, (qi + 2) // 2, pair, init)
    lam = _lambda(lam_ref, lam_init)
    ot = acc_ref[0] * (1.0 / carry[0][1]) - lam * (acc_ref[1] * (1.0 / carry[1][1]))
    ms = jnp.mean(ot * ot, axis=0, keepdims=True)
    yt = ot * lax.rsqrt(ms + EPS) * sgt_ref[...] * (1.0 - lam_init)
    o_ref[0] = yt.T.astype(o_ref.dtype)


def _attn_prompt(q3, k3, vt5, lam4, sgt, lam_init, tq):
    b, t, w = q3.shape
    n_heads = vt5.shape[1]
    hw = w // n_heads
    blk = lambda bi, h, i: (bi, i, h)
    return pl.pallas_call(
        functools.partial(_attn_prompt_kernel, lam_init), grid=(b, n_heads, t // tq),
        in_specs=[pl.BlockSpec((1, tq, hw), blk),
                  pl.BlockSpec((1, t, hw), lambda bi, h, i: (bi, 0, h)),
                  pl.BlockSpec((1, 1) + vt5.shape[2:], lambda bi, h, i: (bi, h, 0, 0, 0)),
                  _full(lam4.shape), _full(sgt.shape)],
        out_specs=pl.BlockSpec((1, tq, hw), blk),
        out_shape=jax.ShapeDtypeStruct((b, t, w), BF16),
        scratch_shapes=[pltpu.VMEM((2, vt5.shape[3], tq), F32), pltpu.VMEM((2, KV_BLOCK, tq), F32),
                        pltpu.VMEM((2, KV_BLOCK, tq), F32)],
        compiler_params=_cparams(("parallel", "parallel", "arbitrary")),
        name="attn_prompt")(q3, k3, vt5, lam4, sgt)


def _attn_sample_kernel(lam_init, n_heads, q_ref, ckt_ref, cv_ref, kn_ref, vn_ref, lam_ref, sg_ref, o_ref):
    past = ckt_ref.shape[2]
    hw = q_ref.shape[2] // n_heads
    lam = _lambda(lam_ref, lam_init)
    for h in range(n_heads):
        cols = slice(h * hw, (h + 1) * hw)
        qz = _split_maps(q_ref[0, :, cols])
        ckt = ckt_ref[0, cols, :].astype(BF16)
        cv = cv_ref[0, pl.ds(h, past, stride=n_heads), :].astype(BF16)
        kn = kn_ref[0, :, cols].astype(BF16)
        vn = vn_ref[0, :, cols].astype(BF16)
        outs = []
        for m in range(2):
            sp = jnp.dot(qz[m], ckt, preferred_element_type=F32)
            sn = _nt(qz[m], kn)
            mx = jnp.maximum(jnp.max(sp, axis=-1, keepdims=True), jnp.max(sn, axis=-1, keepdims=True))
            pp = jnp.exp2(sp - mx)
            pn = jnp.exp2(sn - mx)
            l = jnp.sum(pp, axis=-1, keepdims=True) + jnp.sum(pn, axis=-1, keepdims=True)
            acc = (jnp.dot(pp.astype(BF16), cv, preferred_element_type=F32)
                   + jnp.dot(pn.astype(BF16), vn, preferred_element_type=F32))
            outs.append(acc / l)
        o = outs[0] - lam * outs[1]
        y = _rms(o, sg_ref[...]) * (1.0 - lam_init)
        o_ref[0, :, cols] = y.astype(o_ref.dtype)


def _attn_sample(q3, ckt3, cv3, kn3, vn3, lam4, subln_g, lam_init, n_heads):
    b, t, w = q3.shape
    idx = lambda bi: (bi, 0, 0)
    return pl.pallas_call(
        functools.partial(_attn_sample_kernel, lam_init, n_heads), grid=(b,),
        in_specs=[pl.BlockSpec((1, t, w), idx), pl.BlockSpec((1,) + ckt3.shape[1:], idx),
                  pl.BlockSpec((1,) + cv3.shape[1:], idx), pl.BlockSpec((1, t, w), idx),
                  pl.BlockSpec((1, t, w), idx), _full(lam4.shape), _full(subln_g.shape)],
        out_specs=pl.BlockSpec((1, t, w), idx),
        out_shape=jax.ShapeDtypeStruct((b, t, w), BF16),
        compiler_params=_cparams(("parallel",)),
        name="attn_sample")(q3, ckt3, cv3, kn3, vn3, lam4, subln_g)


def _mid_kernel(n_groups, epg, x_ref, gc_ref, ga_ref, o_ref, wao_ref, wout_ref, g2_ref, wr2_ref, wrh_ref,
                br_ref, x1_ref, route_ref):
    attn = jnp.dot(o_ref[...], wao_ref[...], preferred_element_type=F32)
    merged = gc_ref[...].astype(F32) + ga_ref[...].astype(F32) * attn
    x1 = x_ref[...] + jnp.dot(merged.astype(BF16), wout_ref[...], preferred_element_type=F32)
    x1_ref[...] = x1
    xn2 = _rms(x1, g2_ref[...])

    n_exp = n_groups * epg
    hi = xn2.astype(BF16)
    lo = (xn2 - hi.astype(F32)).astype(BF16)
    both = jnp.dot(hi, wr2_ref[...], preferred_element_type=F32)
    lg = (both[:, :LANES] + both[:, LANES:]
          + jnp.dot(lo, wrh_ref[...], preferred_element_type=F32) + br_ref[...])
    lane = lax.broadcasted_iota(I32, lg.shape, 1)
    neg = -jnp.inf
    big = jnp.int32(2 ** 30)
    is_g = (lane >= n_exp) & (lane < n_exp + n_groups)
    gl = jnp.where(is_g, lg, neg)
    gmax = jnp.max(gl, axis=-1, keepdims=True)
    g_idx = jnp.min(jnp.where(gl == gmax, lane, big), axis=-1, keepdims=True) - n_exp
    p_g = 1.0 / jnp.sum(jnp.where(is_g, jnp.exp(gl - gmax), 0.0), axis=-1, keepdims=True)
    in_grp = (lane >= g_idx * epg) & (lane < (g_idx + 1) * epg)
    el = jnp.where(in_grp, lg, neg)
    v1 = jnp.max(el, axis=-1, keepdims=True)
    e1 = jnp.min(jnp.where(el == v1, lane, big), axis=-1, keepdims=True)
    el2 = jnp.where(lane == e1, neg, el)
    v2 = jnp.max(el2, axis=-1, keepdims=True)
    e2 = jnp.min(jnp.where(el2 == v2, lane, big), axis=-1, keepdims=True)
    t2 = jnp.exp(v2 - v1)
    den = 1.0 + t2
    w1 = p_g / den
    w2 = p_g * t2 / den
    route = jnp.where(lane == 0, e1.astype(F32),
                      jnp.where(lane == 1, e2.astype(F32),
                                jnp.where(lane == 2, w1, jnp.where(lane == 3, w2, 0.0))))
    route_ref[...] = route


def _mid(x2, gc2, gates2, o2, w_ao_bf, w_out_bf, g2, wr2, wrh, br, n_groups, epg, tm):
    n, d = x2.shape
    wo = o2.shape[1]
    row = lambda i: (i, 0)
    return pl.pallas_call(
        functools.partial(_mid_kernel, n_groups, epg), grid=(n // tm,),
        in_specs=[pl.BlockSpec((tm, d), row), pl.BlockSpec((tm, d), row),
                  pl.BlockSpec((tm, d), lambda i: (i, 1)), pl.BlockSpec((tm, wo), row),
                  _full(w_ao_bf.shape, True), _full(w_out_bf.shape, True), _full((1, d)),
                  _full(wr2.shape, True), _full(wrh.shape, True), _full(br.shape)],
        out_specs=[pl.BlockSpec((tm, d), row), pl.BlockSpec((tm, LANES), row)],
        out_shape=[jax.ShapeDtypeStruct((n, d), F32), jax.ShapeDtypeStruct((n, LANES), F32)],
        compiler_params=_cparams(("parallel",)),
        name="mid")(x2, gc2, gates2, o2, w_ao_bf, w_out_bf, g2, wr2, wrh, br)


def _dispatch_kernel(slot_ref, x1_ref, g2_ref, xs_in_hbm, xs_hbm, xn_ref, sem):
    del xs_in_hbm
    tmd = x1_ref.shape[0]
    xn_ref[...] = _rms(x1_ref[...], g2_ref[...])

    def row_copy(r, s):
        return pltpu.make_async_copy(xn_ref.at[pl.ds(r, 1)], xs_hbm.at[pl.ds(s, 1)], sem)

    def issue(r, carry):
        row_copy(r, slot_ref[0, 0, 2 * r]).start()
        row_copy(r, slot_ref[0, 0, 2 * r + 1]).start()
        return carry

    def drain(r, carry):
        row_copy(0, 0).wait()
        row_copy(0, 0).wait()
        return carry

    lax.fori_loop(0, tmd, issue, 0)
    lax.fori_loop(0, tmd, drain, 0)


def _dispatch(slots, x1, g2, xs, tmd):
    n, d = x1.shape
    nt = n // tmd
    return pl.pallas_call(
        _dispatch_kernel, grid=(nt,),
        in_specs=[pl.BlockSpec((1, 1, 2 * tmd), lambda i: (i, 0, 0), memory_space=pltpu.SMEM),
                  pl.BlockSpec((tmd, d), lambda i: (i, 0)), _full((1, d)),
                  pl.BlockSpec(memory_space=pl.ANY)],
        out_specs=pl.BlockSpec(memory_space=pl.ANY),
        out_shape=jax.ShapeDtypeStruct(xs.shape, xs.dtype),
        scratch_shapes=[pltpu.VMEM((tmd, d), F32), pltpu.SemaphoreType.DMA],
        input_output_aliases={3: 0},
        compiler_params=_cparams(("arbitrary",)),
        name="dispatch")(slots.reshape(nt, 1, 2 * tmd), x1, g2, xs)


def _expert_kernel(te_ref, nt_ref, xs_ref, wg_ref, wu_ref, wd_ref, ys_ref):
    del te_ref
    i = pl.program_id(0)

    @pl.when(i < nt_ref[0])
    def _():
        x = xs_ref[...].astype(BF16)
        hg = jnp.dot(x, wg_ref[0], preferred_element_type=F32)
        hu = jnp.dot(x, wu_ref[0], preferred_element_type=F32)
        h = (hg * jax.nn.sigmoid(hg)) * hu
        ys_ref[...] = jnp.dot(h.astype(BF16), wd_ref[0], preferred_element_type=F32)

    @pl.when(i >= nt_ref[0])
    def _():
        ys_ref[...] = jnp.zeros_like(ys_ref)


def _experts(tile_expert, n_tiles, xs, wg_bf, wu_bf, wd_bf, tme):
    s, d = xs.shape
    de = wg_bf.shape[2]
    grid_spec = pltpu.PrefetchScalarGridSpec(
        num_scalar_prefetch=2, grid=(s // tme,),
        in_specs=[pl.BlockSpec((tme, d), lambda i, te, nt: (i, 0)),
                  pl.BlockSpec((1, d, de), lambda i, te, nt: (te[i], 0, 0)),
                  pl.BlockSpec((1, d, de), lambda i, te, nt: (te[i], 0, 0)),
                  pl.BlockSpec((1, de, d), lambda i, te, nt: (te[i], 0, 0))],
        out_specs=pl.BlockSpec((tme, d), lambda i, te, nt: (i, 0)))
    return pl.pallas_call(
        _expert_kernel, grid_spec=grid_spec,
        out_shape=jax.ShapeDtypeStruct((s, d), F32),
        compiler_params=_cparams(("arbitrary",)),
        name="experts")(tile_expert, n_tiles, xs, wg_bf, wu_bf, wd_bf)


def _combine_kernel(slot_ref, x1_ref, route_ref, gf_ref, ys_hbm, y_ref, buf_ref, sem):
    tmf = x1_ref.shape[0]

    def row_copy(s, k, r):
        return pltpu.make_async_copy(ys_hbm.at[pl.ds(s, 1)], buf_ref.at[k, pl.ds(r, 1)], sem)

    def issue(r, carry):
        row_copy(slot_ref[0, 0, 2 * r], 0, r).start()
        row_copy(slot_ref[0, 0, 2 * r + 1], 1, r).start()
        return carry

    def drain(r, carry):
        row_copy(0, 0, 0).wait()
        row_copy(0, 0, 0).wait()
        return carry

    lax.fori_loop(0, tmf, issue, 0)
    lax.fori_loop(0, tmf, drain, 0)
    w1 = route_ref[:, 2:3]
    w2 = route_ref[:, 3:4]
    x2 = x1_ref[...] + (w1 * buf_ref[0] + w2 * buf_ref[1])
    y_ref[...] = _rms(x2, gf_ref[...])


def _combine(slots, x1, route, gf, ys, tmf):
    n, d = x1.shape
    nt = n // tmf
    row = lambda i: (i, 0)
    return pl.pallas_call(
        _combine_kernel, grid=(nt,),
        in_specs=[pl.BlockSpec((1, 1, 2 * tmf), lambda i: (i, 0, 0), memory_space=pltpu.SMEM),
                  pl.BlockSpec((tmf, d), row), pl.BlockSpec((tmf, LANES), row), _full((1, d)),
                  pl.BlockSpec(memory_space=pl.ANY)],
        out_specs=pl.BlockSpec((tmf, d), row),
        out_shape=jax.ShapeDtypeStruct((n, d), F32),
        scratch_shapes=[pltpu.VMEM((2, tmf, d), F32), pltpu.SemaphoreType.DMA],
        compiler_params=_cparams(("arbitrary",)),
        name="combine")(slots.reshape(nt, 1, 2 * tmf), x1, route, gf, ys)


def _pick(n, pref):
    t = min(n, pref)
    while n % t:
        t //= 2
    return t


def _mixer(x3, hist, k_past_t, v_past, pos, p, n_heads, lam_init):
    b, t, d = x3.shape
    n = b * t
    x2 = x3.reshape(n, d)
    c_conv = p["w_dw"].shape[1]
    prompt = k_past_t is None
    tm = _pick(n, 512)
    tabs = _rope_tables(pos, max(t, tm))
    q_scale = (d // n_heads // 4) ** -0.5 * LOG2E
    outs = _in_proj(x2, p["norm1_g"], p["w_cat"], p["b_cat"], tabs, q_scale, tm, t, prompt)
    if prompt:
        glu, q, k_bf, k_t, v, v_t, gates = outs
    else:
        glu, q, k, v, gates = outs

    glu3 = glu.reshape(b, t, c_conv)
    gates3 = gates.reshape(b, t, 2 * d)
    gc = _conv_branch(glu3, hist, p["w_dw"], p["b_dw"], p["ln_g"], p["ln_b"], p["w_co"], p["b_co"],
                      gates3, _pick(t, 256))

    w = q.shape[1]
    q3 = q.reshape(b, t, w)
    hd = w // n_heads // 2
    if prompt:
        o = _attn_prompt(q3, k_bf.reshape(b, t, w), v_t, p["lam4"], p["subln_g"].reshape(-1, 1), lam_init,
                         _pick(t, 256))
        k_out = k_t.reshape(b, n_heads, 2, hd, t).transpose(0, 4, 1, 2, 3)[None]
    else:
        k3 = k.reshape(b, t, w)
        o = _attn_sample(q3, k_past_t, v_past, k3, v.reshape(b, t, w), p["lam4"], p["subln_g"], lam_init, n_heads)
        k_out = k3.reshape(1, b, t, n_heads, 2, hd)
    v_out = v.reshape(1, b, t, n_heads, w // n_heads)

    x1, route = _mid(x2, gc.reshape(n, d), gates, o.reshape(n, w), p["w_ao"], p["w_out"],
                     p["norm2_g"], p["wr2"], p["wrh"], p["br"], p["n_groups"], p["epg"], _pick(n, 512))
    new_hist = glu3[:, t - (p["w_dw"].shape[0] - 1):, :]
    return x1, route, k_out, v_out, new_hist[None]


def kernel(x_prompt, x_sample, cache_k, cache_v, state_conv, norm1_g, w_in, b_glu, w_dw, b_dw, conv_ln_g, conv_ln_b, w_conv_out, b_conv_out, lambda_q1, lambda_k1, lambda_q2, lambda_k2, subln_g, w_attn_out, w_out, norm2_g, w_router_group, b_router_group, w_router_expert, b_router_expert, w_exp_gate, w_exp_up, w_exp_down, final_norm_g):
    bp, tp, d = x_prompt.shape
    bs, ts, _ = x_sample.shape
    assert w_in.shape[0] == 1
    l = 0
    past = cache_k.shape[2]
    n_heads = cache_k.shape[3]
    hd = cache_k.shape[5]
    v_dim = cache_v.shape[4]
    conv_w = w_dw.shape[1]
    c_conv = w_dw.shape[2]
    n_groups = w_router_group.shape[2]
    n_exp = w_router_expert.shape[2]
    epg = n_exp // n_groups
    lam_init = 0.8 - 0.6 * math.exp(-0.3 * l)
    assert c_conv == 1024 and n_heads * 2 * hd == 1024 and n_heads * v_dim == 1024 and tp % 256 == 0

    hc = c_conv // 2
    wl = w_in[l]
    w_cat = jnp.concatenate([wl[:, :hc], wl[:, c_conv:c_conv + hc], wl[:, hc:c_conv], wl[:, c_conv + hc:]],
                            axis=1).astype(BF16)
    bg = b_glu[l]
    b_cat = jnp.concatenate([bg[:hc], bg[c_conv:c_conv + hc], bg[hc:c_conv], bg[c_conv + hc:]])[None]

    wr = jnp.zeros((d, LANES), F32)
    wr = wr.at[:, :n_exp].set(w_router_expert[l]).at[:, n_exp:n_exp + n_groups].set(w_router_group[l])
    wrh = wr.astype(BF16)
    wrl = (wr - wrh.astype(F32)).astype(BF16)
    br = jnp.zeros((1, LANES), F32)
    br = br.at[0, :n_exp].set(b_router_expert[l]).at[0, n_exp:n_exp + n_groups].set(b_router_group[l])
    p = dict(
        norm1_g=norm1_g[l][None], w_cat=w_cat, b_cat=b_cat,
        w_dw=w_dw[l], b_dw=b_dw[l][None], ln_g=conv_ln_g[l][None], ln_b=conv_ln_b[l][None],
        w_co=w_conv_out[l].astype(BF16), b_co=b_conv_out[l][None],
        lam4=jnp.stack([lambda_q1[l], lambda_k1[l], lambda_q2[l], lambda_k2[l]]),
        subln_g=subln_g[l][None], w_ao=w_attn_out[l].astype(BF16), w_out=w_out[l].astype(BF16),
        norm2_g=norm2_g[l][None], wr2=jnp.concatenate([wrh, wrl], axis=1), wrh=wrh, br=br,
        n_groups=n_groups, epg=epg)

    hist_p = jnp.zeros((bp, HIST_PAD, c_conv), F32)
    hist_s = jnp.concatenate(
        [jnp.zeros((bs, HIST_PAD - (conv_w - 1), c_conv), F32), state_conv[l]], axis=1)
    ckt = cache_k[l].transpose(0, 2, 3, 4, 1).reshape(bs, n_heads * 2 * hd, past)
    cv = cache_v[l].reshape(bs, past * n_heads, v_dim)

    x1p, routep, kp, vp, cp = _mixer(x_prompt, hist_p, None, None, jnp.arange(tp), p, n_heads, lam_init)
    x1s, routes, ks, vs, cs = _mixer(x_sample, hist_s, ckt, cv, past + jnp.arange(ts), p, n_heads, lam_init)

    np_, ns_ = bp * tp, bs * ts
    n = np_ + ns_
    tme = 256
    e_idx = jnp.concatenate([routep[:, :2], routes[:, :2]], axis=0).astype(I32)
    onehot = (e_idx[:, :, None] == jnp.arange(n_exp, dtype=I32)[None, None, :]).astype(I32)
    per_tok = onehot.sum(axis=1)
    before = jnp.cumsum(per_tok, axis=0) - per_tok
    rank = jnp.sum(onehot * before[:, None, :], axis=-1)
    counts = per_tok.sum(axis=0)
    tiles_per = (counts + tme - 1) // tme
    tile_end = jnp.cumsum(tiles_per)
    tile_start = tile_end - tiles_per
    slots = (jnp.sum(onehot * (tile_start * tme)[None, None, :], axis=-1) + rank).astype(I32)
    max_tiles = (2 * n) // tme + n_exp
    tile_ids = jnp.arange(max_tiles, dtype=I32)
    tile_expert = jnp.minimum(jnp.sum(tile_ids[:, None] >= tile_end[None, :], axis=1), n_exp - 1).astype(I32)
    n_tiles = tile_end[-1:].astype(I32)

    g2 = norm2_g[l][None]
    xs = jnp.zeros((max_tiles * tme, d), F32)
    xs = _dispatch(slots[:np_].reshape(-1), x1p, g2, xs, _pick(np_, 256))
    xs = _dispatch(slots[np_:].reshape(-1), x1s, g2, xs, _pick(ns_, 256))
    ys = _experts(tile_expert, n_tiles, xs, w_exp_gate[l].astype(BF16), w_exp_up[l].astype(BF16),
                  w_exp_down[l].astype(BF16), tme)
    gf = final_norm_g[None]
    yp = _combine(slots[:np_].reshape(-1), x1p, routep, gf, ys, _pick(np_, 256))
    ysm = _combine(slots[np_:].reshape(-1), x1s, routes, gf, ys, _pick(ns_, 256))

    return (yp.reshape(bp, tp, d), ysm.reshape(bs, ts, d), kp, vp, cp, ks, vs, cs)
```

```python
import functools
import math

import jax
import jax.numpy as jnp
from jax import lax
from jax.experimental import pallas as pl
from jax.experimental.pallas import tpu as pltpu

F32 = jnp.float32
BF16 = jnp.bfloat16
I32 = jnp.int32

EPS = 1e-6
CHUNK = 64
ROPE_DIM = 16
ROPE_THETA = 500000.0
LOG2E = 1.4426950408889634
LANES = 128
SUBLANES = 8
KV_BLOCK = 256
HIST_PAD = 32
VMEM_LIMIT = 56 * 1024 * 1024


def _cparams(sem):
    return pltpu.CompilerParams(dimension_semantics=sem, vmem_limit_bytes=VMEM_LIMIT)


def _full(shape, single=False):
    idx = lambda *_: (0,) * len(shape)
    if single:
        return pl.BlockSpec(shape, idx, pipeline_mode=pl.Buffered(1))
    return pl.BlockSpec(shape, idx)


def _rms(x, g):
    return x * lax.rsqrt(jnp.mean(x * x, axis=-1, keepdims=True) + EPS) * g


def _rms_rows(x_ref, g_ref, xn_ref):
    tm = x_ref.shape[0]
    rc = min(tm, 128)

    def body(r, carry):
        r0 = pl.multiple_of(r * rc, rc)
        xn_ref[pl.ds(r0, rc), :] = _rms(x_ref[pl.ds(r0, rc), :], g_ref[...]).astype(xn_ref.dtype)
        return carry

    lax.fori_loop(0, tm // rc, body, 0)


def _rope(z, c_ref, s1_ref, s2_ref):
    outs = []
    for c in range(z.shape[1] // LANES):
        zc = z[:, c * LANES:(c + 1) * LANES]
        up = pltpu.roll(zc, LANES - ROPE_DIM // 2, axis=1)
        dn = pltpu.roll(zc, ROPE_DIM // 2, axis=1)
        outs.append(zc * c_ref[...] + up * s1_ref[...] + dn * s2_ref[...])
    return jnp.concatenate(outs, axis=1)


def _in_kernel(prompt, q_scale, x_ref, g_ref, w_ref, b_ref, c_ref, s1_ref, s2_ref, *rest):
    if prompt:
        glu_ref, q_ref, k_ref, kt_ref, v_ref, vt_ref, gate_ref, xn_ref = rest
    else:
        glu_ref, q_ref, k_ref, v_ref, gate_ref, xn_ref = rest
    j = pl.program_id(1)
    tm, tn = x_ref.shape[0], w_ref.shape[1]
    cw = 2 * LANES
    nchunk = tn // cw

    @pl.when(j == 0)
    def _():
        _rms_rows(x_ref, g_ref, xn_ref)

    def zchunk(c0):
        return jnp.dot(xn_ref[...], w_ref[:, c0:c0 + cw], preferred_element_type=F32)

    @pl.when(j < 2)
    def _():
        half = tn // 2
        for c in range(half // cw):
            a = zchunk(c * cw) + b_ref[:, c * cw:(c + 1) * cw]
            b = zchunk(half + c * cw) + b_ref[:, half + c * cw:half + (c + 1) * cw]
            glu_ref[:, c * cw:(c + 1) * cw] = a * jax.nn.sigmoid(b)

    @pl.when(j == 2)
    def _():
        for c in range(nchunk):
            zr = _rope(zchunk(c * cw), c_ref, s1_ref, s2_ref)
            q_ref[:, c * cw:(c + 1) * cw] = (zr * q_scale).astype(q_ref.dtype)

    @pl.when(j == 3)
    def _():
        for c in range(nchunk):
            zr = _rope(zchunk(c * cw), c_ref, s1_ref, s2_ref)
            k_ref[:, c * cw:(c + 1) * cw] = zr.astype(k_ref.dtype)
            if prompt:
                kt_ref[0, c * cw:(c + 1) * cw, :] = zr.T

    @pl.when(j == 4)
    def _():
        for c in range(nchunk):
            z = zchunk(c * cw)
            v_ref[:, c * cw:(c + 1) * cw] = z
            if prompt:
                kb = vt_ref.shape[4]
                for hh in range(cw // LANES):
                    for blk in range(tm // kb):
                        vt_ref[0, c * (cw // LANES) + hh, blk] = (
                            z[blk * kb:(blk + 1) * kb, hh * LANES:(hh + 1) * LANES].T.astype(BF16))

    @pl.when(j >= 5)
    def _():
        for c in range(nchunk):
            gate_ref[:, c * cw:(c + 1) * cw] = jax.nn.sigmoid(zchunk(c * cw)).astype(gate_ref.dtype)


def _in_proj(x2, g, w_cat, b_cat, tabs, q_scale, tm, seq, prompt):
    n, d = x2.shape
    tn = 1024
    nj = w_cat.shape[1] // tn
    ngate = nj - 5
    nper = tabs[0].shape[0] // tm
    row0 = lambda i, j: (i, 0)
    in_specs = [pl.BlockSpec((tm, d), row0), _full((1, d)),
                pl.BlockSpec((d, tn), lambda i, j: (0, j)),
                pl.BlockSpec((1, tn), lambda i, j: (0, jnp.minimum(j, 1)))]
    in_specs += [pl.BlockSpec((tm, LANES), lambda i, j: (i % nper, 0))] * 3
    glu_spec = pl.BlockSpec((tm, tn // 2), lambda i, j: (i, jnp.minimum(j, 1)))
    row_spec = pl.BlockSpec((tm, tn), row0)
    gate_spec = pl.BlockSpec((tm, tn), lambda i, j: (i, jnp.clip(j - 5, 0, ngate - 1)))
    sds = jax.ShapeDtypeStruct
    if prompt:
        tpb = seq // tm
        nb = n // seq
        out_specs = [glu_spec, row_spec, row_spec,
                     pl.BlockSpec((1, tn, tm), lambda i, j: (i // tpb, 0, i % tpb)),
                     row_spec,
                     pl.BlockSpec((1, tn // LANES, tm // KV_BLOCK, LANES, KV_BLOCK),
                                  lambda i, j: (i // tpb, 0, i % tpb, 0, 0)),
                     gate_spec]
        out_shape = [sds((n, tn), F32), sds((n, tn), BF16), sds((n, tn), BF16), sds((nb, tn, seq), F32),
                     sds((n, tn), F32), sds((nb, tn // LANES, seq // KV_BLOCK, LANES, KV_BLOCK), BF16),
                     sds((n, ngate * tn), BF16)]
    else:
        out_specs = [glu_spec, row_spec, row_spec, row_spec, gate_spec]
        out_shape = [sds((n, tn), F32), sds((n, tn), BF16), sds((n, tn), F32), sds((n, tn), F32),
                     sds((n, ngate * tn), BF16)]
    return pl.pallas_call(
        functools.partial(_in_kernel, prompt, q_scale), grid=(n // tm, nj),
        in_specs=in_specs, out_specs=out_specs, out_shape=out_shape,
        scratch_shapes=[pltpu.VMEM((tm, d), BF16)],
        compiler_params=_cparams(("parallel", "arbitrary")),
        name="in_proj")(x2, g, w_cat, b_cat, *tabs)


def _rope_tables(pos, rows):
    half = ROPE_DIM // 2
    inv_freq = 1.0 / (ROPE_THETA ** (jnp.arange(0, ROPE_DIM, 2, dtype=F32) / ROPE_DIM))
    ang = pos.astype(F32)[:, None] * inv_freq[None, :]
    cos, sin = jnp.cos(ang), jnp.sin(ang)
    t = pos.shape[0]
    sub = 64
    ones = jnp.ones((t, sub - ROPE_DIM), F32)
    zeros = jnp.zeros((t, sub - ROPE_DIM), F32)
    zh = jnp.zeros((t, half), F32)
    c = jnp.concatenate([cos, cos, ones], axis=1)
    s1 = jnp.concatenate([-sin, zh, zeros], axis=1)
    s2 = jnp.concatenate([zh, sin, zeros], axis=1)
    reps = (rows // t, LANES // sub)
    return tuple(jnp.tile(a, reps) for a in (c, s1, s2))


def _conv_kernel(conv_w, glu_ref, hist_ref, wdw_ref, bdw_ref, lng_ref, lnb_ref, wco_ref, bco_ref,
                 gate_ref, o_ref, xs_ref, acc_ref, c_ref):
    tt = glu_ref.shape[1]
    rows = HIST_PAD + tt
    i = pl.program_id(1)

    @pl.when(i == 0)
    def _():
        xs_ref[0, 0:HIST_PAD, :] = hist_ref[0]

    @pl.when(i > 0)
    def _():
        xs_ref[0, 0:HIST_PAD, :] = xs_ref[0, tt:tt + HIST_PAD, :]

    xs_ref[0, HIST_PAD:rows, :] = glu_ref[0]
    for b in range(1, SUBLANES):
        xs_ref[b, 0:rows - SUBLANES, :] = xs_ref[0, b:b + rows - SUBLANES, :]

    rc = 2 * SUBLANES
    off = HIST_PAD - (conv_w - 1)

    def chunk(r, carry):
        r0 = pl.multiple_of(r * rc, rc)
        accs = [None] * (rc // SUBLANES)
        for j in range(conv_w):
            a, b = divmod(off + j, SUBLANES)
            wj = wdw_ref[j]
            for u in range(len(accs)):
                start = pl.multiple_of(r0 + (a + u) * SUBLANES, SUBLANES)
                term = xs_ref[b, pl.ds(start, SUBLANES), :] * wj
                accs[u] = term if accs[u] is None else accs[u] + term
        acc_ref[pl.ds(r0, rc), :] = jnp.concatenate(accs, axis=0)
        return carry

    lax.fori_loop(0, tt // rc, chunk, 0)
    acc = acc_ref[...] + bdw_ref[...]
    mu = jnp.mean(acc, axis=-1, keepdims=True)
    xc = acc - mu
    var = jnp.mean(xc * xc, axis=-1, keepdims=True)
    y = xc * lax.rsqrt(var + EPS) * lng_ref[...] + lnb_ref[...]
    c_ref[...] = (y * jax.nn.sigmoid(y)).astype(BF16)
    out = jnp.dot(c_ref[...], wco_ref[...], preferred_element_type=F32) + bco_ref[...]
    o_ref[0] = (gate_ref[0].astype(F32) * out).astype(o_ref.dtype)


def _conv_branch(glu3, hist, w_dw, b_dw, ln_g, ln_b, w_co_bf, b_co, gates3, tt):
    b, t, c = glu3.shape
    d = w_co_bf.shape[1]
    conv_w = w_dw.shape[0]
    return pl.pallas_call(
        functools.partial(_conv_kernel, conv_w), grid=(b, t // tt),
        in_specs=[pl.BlockSpec((1, tt, c), lambda bi, i: (bi, i, 0)),
                  pl.BlockSpec((1, HIST_PAD, c), lambda bi, i: (bi, 0, 0)),
                  _full((conv_w, SUBLANES, c)), _full((1, c)), _full((1, c)), _full((1, c)),
                  _full((c, d)), _full((1, d)),
                  pl.BlockSpec((1, tt, d), lambda bi, i: (bi, i, 0))],
        out_specs=pl.BlockSpec((1, tt, d), lambda bi, i: (bi, i, 0)),
        out_shape=jax.ShapeDtypeStruct((b, t, d), BF16),
        scratch_shapes=[pltpu.VMEM((SUBLANES, HIST_PAD + tt, c), F32), pltpu.VMEM((tt, c), F32),
                        pltpu.VMEM((tt, c), BF16)],
        compiler_params=_cparams(("parallel", "arbitrary")),
        name="conv_branch")(glu3, hist, jnp.broadcast_to(w_dw[:, None, :], (conv_w, SUBLANES, c)), b_dw, ln_g, ln_b,
                            w_co_bf, b_co, gates3)


def _lambda(lam_ref, lam_init):
    lq1, lk1, lq2, lk2 = (lam_ref[r:r + 1, :] for r in range(4))
    return (jnp.exp(jnp.sum(lq1 * lk1, axis=-1, keepdims=True))
            - jnp.exp(jnp.sum(lq2 * lk2, axis=-1, keepdims=True)) + lam_init)


def _split_maps(q):
    lane = lax.broadcasted_iota(I32, q.shape, 1)
    zero = jnp.zeros_like(q)
    half = q.shape[1] // 2
    return jnp.where(lane < half, q, zero), jnp.where(lane >= half, q, zero)


def _nt(a, b):
    return lax.dot_general(a, b, (((1,), (1,)), ((), ())), preferred_element_type=F32)


def _attn_prompt_kernel(lam_init, q_ref, k_ref, vt_ref, lam_ref, sgt_ref, o_ref, acc_ref, sa_ref, sb_ref):
    tq = q_ref.shape[1]
    tk = vt_ref.shape[4]
    assert tq == tk
    qi = pl.program_id(2)
    qz = _split_maps(q_ref[0])
    acc_ref[...] = jnp.zeros_like(acc_ref)
    cdiff = (lax.broadcasted_iota(I32, (tk, tq), 0) // CHUNK) - (lax.broadcasted_iota(I32, (tk, tq), 1) // CHUNK)

    def scores(kb, s_ref):
        k0 = pl.multiple_of(kb * tk, tk)
        kblk = k_ref[0, pl.ds(k0, tk), :]
        for m in range(2):
            s_ref[m] = _nt(kblk, qz[m])

    def consume(kb, s_ref, carry, diagonal):
        vblk = vt_ref[0, 0, kb]
        new, alphas, ps = [], [], []
        for m in range(2):
            mx, l = carry[m]
            sm = jnp.where(cdiff <= 0, s_ref[m], -jnp.inf) if diagonal else s_ref[m]
            mn = jnp.maximum(mx, jnp.max(sm, axis=0, keepdims=True))
            alpha = jnp.exp2(mx - mn)
            p = jnp.exp2(sm - mn)
            new.append((mn, alpha * l + jnp.sum(p, axis=0, keepdims=True)))
            alphas.append(alpha)
            ps.append(p.astype(BF16))
        pv = [jnp.dot(vblk, ps[m], preferred_element_type=F32) for m in range(2)]
        for m in range(2):
            acc_ref[m] = alphas[m] * acc_ref[m] + pv[m]
        return tuple(new)

    def pair(j, carry):
        scores(2 * j + 1, sb_ref)
        carry = consume(2 * j, sa_ref, carry, False)
        scores(2 * j + 2, sa_ref)
        return consume(2 * j + 1, sb_ref, carry, False)

    def tail_one(carry):
        return consume(qi, sa_ref, carry, True)

    def tail_two(carry):
        scores(qi, sb_ref)
        carry = consume(qi - 1, sa_ref, carry, False)
        return consume(qi, sb_ref, carry, True)

    init = tuple((jnp.full((1, tq), -jnp.inf, F32), jnp.zeros((1, tq), F32)) for _ in range(2))
    scores(0, sa_ref)
    carry = lax.fori_loop(0, qi // 2, pair, init)
    carry = lax.cond(qi % 2 == 0, tail_one, tail_two, carry)
    lam = _lambda(lam_ref, lam_init)
    ot = acc_ref[0] * (1.0 / carry[0][1]) - lam * (acc_ref[1] * (1.0 / carry[1][1]))
    ms = jnp.mean(ot * ot, axis=0, keepdims=True)
    yt = ot * lax.rsqrt(ms + EPS) * sgt_ref[...] * (1.0 - lam_init)
    o_ref[0] = yt.T.astype(o_ref.dtype)


def _attn_prompt(q3, k3, vt5, lam4, sgt, lam_init, tq):
    b, t, w = q3.shape
    n_heads = vt5.shape[1]
    hw = w // n_heads
    blk = lambda bi, h, i: (bi, i, h)
    return pl.pallas_call(
        functools.partial(_attn_prompt_kernel, lam_init), grid=(b, n_heads, t // tq),
        in_specs=[pl.BlockSpec((1, tq, hw), blk),
                  pl.BlockSpec((1, t, hw), lambda bi, h, i: (bi, 0, h)),
                  pl.BlockSpec((1, 1) + vt5.shape[2:], lambda bi, h, i: (bi, h, 0, 0, 0)),
                  _full(lam4.shape), _full(sgt.shape)],
        out_specs=pl.BlockSpec((1, tq, hw), blk),
        out_shape=jax.ShapeDtypeStruct((b, t, w), BF16),
        scratch_shapes=[pltpu.VMEM((2, vt5.shape[3], tq), F32), pltpu.VMEM((2, KV_BLOCK, tq), F32),
                        pltpu.VMEM((2, KV_BLOCK, tq), F32)],
        compiler_params=_cparams(("parallel", "parallel", "arbitrary")),
        name="attn_prompt")(q3, k3, vt5, lam4, sgt)


def _attn_sample_kernel(lam_init, n_heads, q_ref, ckt_ref, cv_ref, kn_ref, vn_ref, lam_ref, sg_ref, o_ref):
    past = ckt_ref.shape[2]
    hw = q_ref.shape[2] // n_heads
    lam = _lambda(lam_ref, lam_init)
    for h in range(n_heads):
        cols = slice(h * hw, (h + 1) * hw)
        qz = _split_maps(q_ref[0, :, cols])
        ckt = ckt_ref[0, cols, :].astype(BF16)
        cv = cv_ref[0, pl.ds(h, past, stride=n_heads), :].astype(BF16)
        kn = kn_ref[0, :, cols].astype(BF16)
        vn = vn_ref[0, :, cols].astype(BF16)
        outs = []
        for m in range(2):
            sp = jnp.dot(qz[m], ckt, preferred_element_type=F32)
            sn = _nt(qz[m], kn)
            mx = jnp.maximum(jnp.max(sp, axis=-1, keepdims=True), jnp.max(sn, axis=-1, keepdims=True))
            pp = jnp.exp2(sp - mx)
            pn = jnp.exp2(sn - mx)
            l = jnp.sum(pp, axis=-1, keepdims=True) + jnp.sum(pn, axis=-1, keepdims=True)
            acc = (jnp.dot(pp.astype(BF16), cv, preferred_element_type=F32)
                   + jnp.dot(pn.astype(BF16), vn, preferred_element_type=F32))
            outs.append(acc / l)
        o = outs[0] - lam * outs[1]
        y = _rms(o, sg_ref[...]) * (1.0 - lam_init)
        o_ref[0, :, cols] = y.astype(o_ref.dtype)


def _attn_sample(q3, ckt3, cv3, kn3, vn3, lam4, subln_g, lam_init, n_heads):
    b, t, w = q3.shape
    idx = lambda bi: (bi, 0, 0)
    return pl.pallas_call(
        functools.partial(_attn_sample_kernel, lam_init, n_heads), grid=(b,),
        in_specs=[pl.BlockSpec((1, t, w), idx), pl.BlockSpec((1,) + ckt3.shape[1:], idx),
                  pl.BlockSpec((1,) + cv3.shape[1:], idx), pl.BlockSpec((1, t, w), idx),
                  pl.BlockSpec((1, t, w), idx), _full(lam4.shape), _full(subln_g.shape)],
        out_specs=pl.BlockSpec((1, t, w), idx),
        out_shape=jax.ShapeDtypeStruct((b, t, w), BF16),
        compiler_params=_cparams(("parallel",)),
        name="attn_sample")(q3, ckt3, cv3, kn3, vn3, lam4, subln_g)


def _mid_kernel(n_groups, epg, x_ref, gc_ref, ga_ref, o_ref, wao_ref, wout_ref, g2_ref, wr2_ref, wrh_ref,
                br_ref, x1_ref, route_ref):
    attn = jnp.dot(o_ref[...], wao_ref[...], preferred_element_type=F32)
    merged = gc_ref[...].astype(F32) + ga_ref[...].astype(F32) * attn
    x1 = x_ref[...] + jnp.dot(merged.astype(BF16), wout_ref[...], preferred_element_type=F32)
    x1_ref[...] = x1
    xn2 = _rms(x1, g2_ref[...])

    n_exp = n_groups * epg
    hi = xn2.astype(BF16)
    lo = (xn2 - hi.astype(F32)).astype(BF16)
    both = jnp.dot(hi, wr2_ref[...], preferred_element_type=F32)
    lg = (both[:, :LANES] + both[:, LANES:]
          + jnp.dot(lo, wrh_ref[...], preferred_element_type=F32) + br_ref[...])
    lane = lax.broadcasted_iota(I32, lg.shape, 1)
    neg = -jnp.inf
    big = jnp.int32(2 ** 30)
    is_g = (lane >= n_exp) & (lane < n_exp + n_groups)
    gl = jnp.where(is_g, lg, neg)
    gmax = jnp.max(gl, axis=-1, keepdims=True)
    g_idx = jnp.min(jnp.where(gl == gmax, lane, big), axis=-1, keepdims=True) - n_exp
    p_g = 1.0 / jnp.sum(jnp.where(is_g, jnp.exp(gl - gmax), 0.0), axis=-1, keepdims=True)
    in_grp = (lane >= g_idx * epg) & (lane < (g_idx + 1) * epg)
    el = jnp.where(in_grp, lg, neg)
    v1 = jnp.max(el, axis=-1, keepdims=True)
    e1 = jnp.min(jnp.where(el == v1, lane, big), axis=-1, keepdims=True)
    el2 = jnp.where(lane == e1, neg, el)
    v2 = jnp.max(el2, axis=-1, keepdims=True)
    e2 = jnp.min(jnp.where(el2 == v2, lane, big), axis=-1, keepdims=True)
    t2 = jnp.exp(v2 - v1)
    den = 1.0 + t2
    w1 = p_g / den
    w2 = p_g * t2 / den
    route = jnp.where(lane == 0, e1.astype(F32),
                      jnp.where(lane == 1, e2.astype(F32),
                                jnp.where(lane == 2, w1, jnp.where(lane == 3, w2, 0.0))))
    route_ref[...] = route


def _mid(x2, gc2, gates2, o2, w_ao_bf, w_out_bf, g2, wr2, wrh, br, n_groups, epg, tm):
    n, d = x2.shape
    wo = o2.shape[1]
    row = lambda i: (i, 0)
    return pl.pallas_call(
        functools.partial(_mid_kernel, n_groups, epg), grid=(n // tm,),
        in_specs=[pl.BlockSpec((tm, d), row), pl.BlockSpec((tm, d), row),
                  pl.BlockSpec((tm, d), lambda i: (i, 1)), pl.BlockSpec((tm, wo), row),
                  _full(w_ao_bf.shape, True), _full(w_out_bf.shape, True), _full((1, d)),
                  _full(wr2.shape, True), _full(wrh.shape, True), _full(br.shape)],
        out_specs=[pl.BlockSpec((tm, d), row), pl.BlockSpec((tm, LANES), row)],
        out_shape=[jax.ShapeDtypeStruct((n, d), F32), jax.ShapeDtypeStruct((n, LANES), F32)],
        compiler_params=_cparams(("parallel",)),
        name="mid")(x2, gc2, gates2, o2, w_ao_bf, w_out_bf, g2, wr2, wrh, br)


def _dispatch_kernel(slot_ref, x1_ref, g2_ref, xs_in_hbm, xs_hbm, xn_ref, sem):
    del xs_in_hbm
    tmd = x1_ref.shape[0]
    xn_ref[...] = _rms(x1_ref[...], g2_ref[...])

    def row_copy(r, s):
        return pltpu.make_async_copy(xn_ref.at[pl.ds(r, 1)], xs_hbm.at[pl.ds(s, 1)], sem)

    for r in range(tmd):
        row_copy(r, slot_ref[0, 0, 2 * r]).start()
        row_copy(r, slot_ref[0, 0, 2 * r + 1]).start()

    def drain(r, carry):
        row_copy(0, 0).wait()
        row_copy(0, 0).wait()
        return carry

    lax.fori_loop(0, tmd, drain, 0)


def _dispatch(slots, x1, g2, xs, tmd):
    n, d = x1.shape
    nt = n // tmd
    return pl.pallas_call(
        _dispatch_kernel, grid=(nt,),
        in_specs=[pl.BlockSpec((1, 1, 2 * tmd), lambda i: (i, 0, 0), memory_space=pltpu.SMEM),
                  pl.BlockSpec((tmd, d), lambda i: (i, 0)), _full((1, d)),
                  pl.BlockSpec(memory_space=pl.ANY)],
        out_specs=pl.BlockSpec(memory_space=pl.ANY),
        out_shape=jax.ShapeDtypeStruct(xs.shape, xs.dtype),
        scratch_shapes=[pltpu.VMEM((tmd, d), F32), pltpu.SemaphoreType.DMA],
        input_output_aliases={3: 0},
        compiler_params=_cparams(("arbitrary",)),
        name="dispatch")(slots.reshape(nt, 1, 2 * tmd), x1, g2, xs)


def _expert_kernel(te_ref, nt_ref, xs_ref, wg_ref, wu_ref, wd_ref, ys_ref, wgb_ref, wub_ref, wdb_ref):
    i = pl.program_id(0)
    used = i < nt_ref[0]
    new_expert = (i == 0) | (te_ref[i] != te_ref[jnp.maximum(i - 1, 0)])

    @pl.when(used & new_expert)
    def _():
        wgb_ref[...] = wg_ref[0].astype(BF16)
        wub_ref[...] = wu_ref[0].astype(BF16)
        wdb_ref[...] = wd_ref[0].astype(BF16)

    @pl.when(used)
    def _():
        x = xs_ref[...].astype(BF16)
        hg = jnp.dot(x, wgb_ref[...], preferred_element_type=F32)
        hu = jnp.dot(x, wub_ref[...], preferred_element_type=F32)
        h = (hg * jax.nn.sigmoid(hg)) * hu
        ys_ref[...] = jnp.dot(h.astype(BF16), wdb_ref[...], preferred_element_type=F32)

    @pl.when(jnp.logical_not(used))
    def _():
        ys_ref[...] = jnp.zeros_like(ys_ref)


def _experts(tile_expert, n_tiles, xs, wg, wu, wd, tme):
    s, d = xs.shape
    de = wg.shape[2]
    tile = lambda i, te, nt: (jnp.minimum(i, nt[0] - 1), 0)
    wmap = lambda i, te, nt: (te[jnp.minimum(i, nt[0] - 1)], 0, 0)
    grid_spec = pltpu.PrefetchScalarGridSpec(
        num_scalar_prefetch=2, grid=(s // tme,),
        in_specs=[pl.BlockSpec((tme, d), tile),
                  pl.BlockSpec((1, d, de), wmap), pl.BlockSpec((1, d, de), wmap), pl.BlockSpec((1, de, d), wmap)],
        out_specs=pl.BlockSpec((tme, d), lambda i, te, nt: (i, 0)),
        scratch_shapes=[pltpu.VMEM((d, de), BF16), pltpu.VMEM((d, de), BF16), pltpu.VMEM((de, d), BF16)])
    return pl.pallas_call(
        _expert_kernel, grid_spec=grid_spec,
        out_shape=jax.ShapeDtypeStruct((s, d), F32),
        compiler_params=_cparams(("arbitrary",)),
        name="experts")(tile_expert, n_tiles, xs, wg, wu, wd)


def _combine_kernel(slot_ref, x1_ref, route_ref, gf_ref, ys_hbm, y_ref, buf_ref, sem):
    tmf = x1_ref.shape[0]

    def row_copy(s, k, r):
        return pltpu.make_async_copy(ys_hbm.at[pl.ds(s, 1)], buf_ref.at[k, pl.ds(r, 1)], sem)

    for r in range(tmf):
        row_copy(slot_ref[0, 0, 2 * r], 0, r).start()
        row_copy(slot_ref[0, 0, 2 * r + 1], 1, r).start()

    def drain(r, carry):
        row_copy(0, 0, 0).wait()
        row_copy(0, 0, 0).wait()
        return carry

    lax.fori_loop(0, tmf, drain, 0)
    w1 = route_ref[:, 2:3]
    w2 = route_ref[:, 3:4]
    x2 = x1_ref[...] + (w1 * buf_ref[0] + w2 * buf_ref[1])
    y_ref[...] = _rms(x2, gf_ref[...])


def _combine(slots, x1, route, gf, ys, tmf):
    n, d = x1.shape
    nt = n // tmf
    row = lambda i: (i, 0)
    return pl.pallas_call(
        _combine_kernel, grid=(nt,),
        in_specs=[pl.BlockSpec((1, 1, 2 * tmf), lambda i: (i, 0, 0), memory_space=pltpu.SMEM),
                  pl.BlockSpec((tmf, d), row), pl.BlockSpec((tmf, LANES), row), _full((1, d)),
                  pl.BlockSpec(memory_space=pl.ANY)],
        out_specs=pl.BlockSpec((tmf, d), row),
        out_shape=jax.ShapeDtypeStruct((n, d), F32),
        scratch_shapes=[pltpu.VMEM((2, tmf, d), F32), pltpu.SemaphoreType.DMA],
        compiler_params=_cparams(("arbitrary",)),
        name="combine")(slots.reshape(nt, 1, 2 * tmf), x1, route, gf, ys)


def _pick(n, pref):
    t = min(n, pref)
    while n % t:
        t //= 2
    return t


def _mixer(x3, hist, k_past_t, v_past, pos, p, n_heads, lam_init):
    b, t, d = x3.shape
    n = b * t
    x2 = x3.reshape(n, d)
    c_conv = p["w_dw"].shape[1]
    prompt = k_past_t is None
    tm = _pick(n, 512)
    tabs = _rope_tables(pos, max(t, tm))
    q_scale = (d // n_heads // 4) ** -0.5 * LOG2E
    outs = _in_proj(x2, p["norm1_g"], p["w_cat"], p["b_cat"], tabs, q_scale, tm, t, prompt)
    if prompt:
        glu, q, k_bf, k_t, v, v_t, gates = outs
    else:
        glu, q, k, v, gates = outs

    glu3 = glu.reshape(b, t, c_conv)
    gates3 = gates.reshape(b, t, 2 * d)
    gc = _conv_branch(glu3, hist, p["w_dw"], p["b_dw"], p["ln_g"], p["ln_b"], p["w_co"], p["b_co"],
                      gates3, _pick(t, 256))

    w = q.shape[1]
    q3 = q.reshape(b, t, w)
    hd = w // n_heads // 2
    if prompt:
        o = _attn_prompt(q3, k_bf.reshape(b, t, w), v_t, p["lam4"], p["subln_g"].reshape(-1, 1), lam_init,
                         _pick(t, 256))
        k_out = k_t.reshape(b, n_heads, 2, hd, t).transpose(0, 4, 1, 2, 3)[None]
    else:
        k3 = k.reshape(b, t, w)
        o = _attn_sample(q3, k_past_t, v_past, k3, v.reshape(b, t, w), p["lam4"], p["subln_g"], lam_init, n_heads)
        k_out = k3.reshape(1, b, t, n_heads, 2, hd)
    v_out = v.reshape(1, b, t, n_heads, w // n_heads)

    x1, route = _mid(x2, gc.reshape(n, d), gates, o.reshape(n, w), p["w_ao"], p["w_out"],
                     p["norm2_g"], p["wr2"], p["wrh"], p["br"], p["n_groups"], p["epg"], _pick(n, 512))
    new_hist = glu3[:, t - (p["w_dw"].shape[0] - 1):, :]
    return x1, route, k_out, v_out, new_hist[None]


def kernel(x_prompt, x_sample, cache_k, cache_v, state_conv, norm1_g, w_in, b_glu, w_dw, b_dw, conv_ln_g, conv_ln_b, w_conv_out, b_conv_out, lambda_q1, lambda_k1, lambda_q2, lambda_k2, subln_g, w_attn_out, w_out, norm2_g, w_router_group, b_router_group, w_router_expert, b_router_expert, w_exp_gate, w_exp_up, w_exp_down, final_norm_g):
    bp, tp, d = x_prompt.shape
    bs, ts, _ = x_sample.shape
    assert w_in.shape[0] == 1
    l = 0
    past = cache_k.shape[2]
    n_heads = cache_k.shape[3]
    hd = cache_k.shape[5]
    v_dim = cache_v.shape[4]
    conv_w = w_dw.shape[1]
    c_conv = w_dw.shape[2]
    n_groups = w_router_group.shape[2]
    n_exp = w_router_expert.shape[2]
    epg = n_exp // n_groups
    lam_init = 0.8 - 0.6 * math.exp(-0.3 * l)
    assert c_conv == 1024 and n_heads * 2 * hd == 1024 and n_heads * v_dim == 1024 and tp % 256 == 0

    hc = c_conv // 2
    wl = w_in[l]
    w_cat = jnp.concatenate([wl[:, :hc], wl[:, c_conv:c_conv + hc], wl[:, hc:c_conv], wl[:, c_conv + hc:]],
                            axis=1).astype(BF16)
    bg = b_glu[l]
    b_cat = jnp.concatenate([bg[:hc], bg[c_conv:c_conv + hc], bg[hc:c_conv], bg[c_conv + hc:]])[None]

    wr = jnp.zeros((d, LANES), F32)
    wr = wr.at[:, :n_exp].set(w_router_expert[l]).at[:, n_exp:n_exp + n_groups].set(w_router_group[l])
    wrh = wr.astype(BF16)
    wrl = (wr - wrh.astype(F32)).astype(BF16)
    br = jnp.zeros((1, LANES), F32)
    br = br.at[0, :n_exp].set(b_router_expert[l]).at[0, n_exp:n_exp + n_groups].set(b_router_group[l])
    p = dict(
        norm1_g=norm1_g[l][None], w_cat=w_cat, b_cat=b_cat,
        w_dw=w_dw[l], b_dw=b_dw[l][None], ln_g=conv_ln_g[l][None], ln_b=conv_ln_b[l][None],
        w_co=w_conv_out[l].astype(BF16), b_co=b_conv_out[l][None],
        lam4=jnp.stack([lambda_q1[l], lambda_k1[l], lambda_q2[l], lambda_k2[l]]),
        subln_g=subln_g[l][None], w_ao=w_attn_out[l].astype(BF16), w_out=w_out[l].astype(BF16),
        norm2_g=norm2_g[l][None], wr2=jnp.concatenate([wrh, wrl], axis=1), wrh=wrh, br=br,
        n_groups=n_groups, epg=epg)

    hist_p = jnp.zeros((bp, HIST_PAD, c_conv), F32)
    hist_s = jnp.concatenate(
        [jnp.zeros((bs, HIST_PAD - (conv_w - 1), c_conv), F32), state_conv[l]], axis=1)
    ckt = cache_k[l].transpose(0, 2, 3, 4, 1).reshape(bs, n_heads * 2 * hd, past)
    cv = cache_v[l].reshape(bs, past * n_heads, v_dim)

    x1p, routep, kp, vp, cp = _mixer(x_prompt, hist_p, None, None, jnp.arange(tp), p, n_heads, lam_init)
    x1s, routes, ks, vs, cs = _mixer(x_sample, hist_s, ckt, cv, past + jnp.arange(ts), p, n_heads, lam_init)

    np_, ns_ = bp * tp, bs * ts
    n = np_ + ns_
    tme = 256
    e_idx = jnp.concatenate([routep[:, :2], routes[:, :2]], axis=0).astype(I32)
    onehot = (e_idx[:, :, None] == jnp.arange(n_exp, dtype=I32)[None, None, :]).astype(I32)
    per_tok = onehot.sum(axis=1)
    before = jnp.cumsum(per_tok, axis=0) - per_tok
    rank = jnp.sum(onehot * before[:, None, :], axis=-1)
    counts = per_tok.sum(axis=0)
    tiles_per = (counts + tme - 1) // tme
    tile_end = jnp.cumsum(tiles_per)
    tile_start = tile_end - tiles_per
    slots = (jnp.sum(onehot * (tile_start * tme)[None, None, :], axis=-1) + rank).astype(I32)
    max_tiles = (2 * n) // tme + n_exp
    tile_ids = jnp.arange(max_tiles, dtype=I32)
    tile_expert = jnp.minimum(jnp.sum(tile_ids[:, None] >= tile_end[None, :], axis=1), n_exp - 1).astype(I32)
    n_tiles = tile_end[-1:].astype(I32)

    g2 = norm2_g[l][None]
    xs = jnp.zeros((max_tiles * tme, d), F32)
    xs = _dispatch(slots[:np_].reshape(-1), x1p, g2, xs, _pick(np_, 256))
    xs = _dispatch(slots[np_:].reshape(-1), x1s, g2, xs, _pick(ns_, 256))
    ys = _experts(tile_expert, n_tiles, xs, w_exp_gate[l], w_exp_up[l], w_exp_down[l], tme)
    gf = final_norm_g[None]
    yp = _combine(slots[:np_].reshape(-1), x1p, routep, gf, ys, _pick(np_, 256))
    ysm = _combine(slots[np_:].reshape(-1), x1s, routes, gf, ys, _pick(ns_, 256))

    return (yp.reshape(bp, tp, d), ysm.reshape(bs, ts, d), kp, vp, cp, ks, vs, cs)
```

```python
import functools
import math

import jax
import jax.numpy as jnp
from jax import lax
from jax.experimental import pallas as pl
from jax.experimental.pallas import tpu as pltpu

F32 = jnp.float32
BF16 = jnp.bfloat16
I32 = jnp.int32

EPS = 1e-6
CHUNK = 64
ROPE_DIM = 16
ROPE_THETA = 500000.0
LOG2E = 1.4426950408889634
LANES = 128
SUBLANES = 8
KV_BLOCK = 256
HIST_PAD = 32
VMEM_LIMIT = 56 * 1024 * 1024


def _cparams(sem):
    return pltpu.CompilerParams(dimension_semantics=sem, vmem_limit_bytes=VMEM_LIMIT)


def _full(shape, single=False):
    idx = lambda *_: (0,) * len(shape)
    if single:
        return pl.BlockSpec(shape, idx, pipeline_mode=pl.Buffered(1))
    return pl.BlockSpec(shape, idx)


def _rms(x, g):
    return x * lax.rsqrt(jnp.mean(x * x, axis=-1, keepdims=True) + EPS) * g


def _rms_rows(x_ref, g_ref, xn_ref):
    tm = x_ref.shape[0]
    rc = min(tm, 128)

    def body(r, carry):
        r0 = pl.multiple_of(r * rc, rc)
        xn_ref[pl.ds(r0, rc), :] = _rms(x_ref[pl.ds(r0, rc), :], g_ref[...]).astype(xn_ref.dtype)
        return carry

    lax.fori_loop(0, tm // rc, body, 0)


def _rope(z, c_ref, s1_ref, s2_ref):
    outs = []
    for c in range(z.shape[1] // LANES):
        zc = z[:, c * LANES:(c + 1) * LANES]
        up = pltpu.roll(zc, LANES - ROPE_DIM // 2, axis=1)
        dn = pltpu.roll(zc, ROPE_DIM // 2, axis=1)
        outs.append(zc * c_ref[...] + up * s1_ref[...] + dn * s2_ref[...])
    return jnp.concatenate(outs, axis=1)


def _in_kernel(prompt, q_scale, x_ref, g_ref, w_ref, b_ref, c_ref, s1_ref, s2_ref, *rest):
    if prompt:
        glu_ref, q_ref, k_ref, kt_ref, v_ref, vt_ref, gate_ref, xn_ref = rest
    else:
        glu_ref, q_ref, k_ref, v_ref, gate_ref, xn_ref = rest
    j = pl.program_id(1)
    tm, tn = x_ref.shape[0], w_ref.shape[1]
    cw = 2 * LANES
    nchunk = tn // cw

    @pl.when(j == 0)
    def _():
        _rms_rows(x_ref, g_ref, xn_ref)

    def zchunk(c0):
        return jnp.dot(xn_ref[...], w_ref[:, c0:c0 + cw], preferred_element_type=F32)

    @pl.when(j < 2)
    def _():
        half = tn // 2
        for c in range(half // cw):
            a = zchunk(c * cw) + b_ref[:, c * cw:(c + 1) * cw]
            b = zchunk(half + c * cw) + b_ref[:, half + c * cw:half + (c + 1) * cw]
            glu_ref[:, c * cw:(c + 1) * cw] = a * jax.nn.sigmoid(b)

    @pl.when(j == 2)
    def _():
        for c in range(nchunk):
            zr = _rope(zchunk(c * cw), c_ref, s1_ref, s2_ref)
            q_ref[:, c * cw:(c + 1) * cw] = (zr * q_scale).astype(q_ref.dtype)

    @pl.when(j == 3)
    def _():
        for c in range(nchunk):
            zr = _rope(zchunk(c * cw), c_ref, s1_ref, s2_ref)
            k_ref[:, c * cw:(c + 1) * cw] = zr.astype(k_ref.dtype)
            if prompt:
                kt_ref[0, c * cw:(c + 1) * cw, :] = zr.T

    @pl.when(j == 4)
    def _():
        for c in range(nchunk):
            z = zchunk(c * cw)
            v_ref[:, c * cw:(c + 1) * cw] = z
            if prompt:
                kb = vt_ref.shape[4]
                for hh in range(cw // LANES):
                    for blk in range(tm // kb):
                        vt_ref[0, c * (cw // LANES) + hh, blk] = (
                            z[blk * kb:(blk + 1) * kb, hh * LANES:(hh + 1) * LANES].T.astype(BF16))

    @pl.when(j >= 5)
    def _():
        for c in range(nchunk):
            gate_ref[:, c * cw:(c + 1) * cw] = jax.nn.sigmoid(zchunk(c * cw)).astype(gate_ref.dtype)


def _in_proj(x2, g, w_cat, b_cat, tabs, q_scale, tm, seq, prompt):
    n, d = x2.shape
    tn = 1024
    nj = w_cat.shape[1] // tn
    ngate = nj - 5
    nper = tabs[0].shape[0] // tm
    row0 = lambda i, j: (i, 0)
    in_specs = [pl.BlockSpec((tm, d), row0), _full((1, d)),
                pl.BlockSpec((d, tn), lambda i, j: (0, j)),
                pl.BlockSpec((1, tn), lambda i, j: (0, jnp.minimum(j, 1)))]
    in_specs += [pl.BlockSpec((tm, LANES), lambda i, j: (i % nper, 0))] * 3
    glu_spec = pl.BlockSpec((tm, tn // 2), lambda i, j: (i, jnp.minimum(j, 1)))
    row_spec = pl.BlockSpec((tm, tn), row0)
    gate_spec = pl.BlockSpec((tm, tn), lambda i, j: (i, jnp.clip(j - 5, 0, ngate - 1)))
    sds = jax.ShapeDtypeStruct
    if prompt:
        tpb = seq // tm
        nb = n // seq
        out_specs = [glu_spec, row_spec, row_spec,
                     pl.BlockSpec((1, tn, tm), lambda i, j: (i // tpb, 0, i % tpb)),
                     row_spec,
                     pl.BlockSpec((1, tn // LANES, tm // KV_BLOCK, LANES, KV_BLOCK),
                                  lambda i, j: (i // tpb, 0, i % tpb, 0, 0)),
                     gate_spec]
        out_shape = [sds((n, tn), F32), sds((n, tn), BF16), sds((n, tn), BF16), sds((nb, tn, seq), F32),
                     sds((n, tn), F32), sds((nb, tn // LANES, seq // KV_BLOCK, LANES, KV_BLOCK), BF16),
                     sds((n, ngate * tn), BF16)]
    else:
        out_specs = [glu_spec, row_spec, row_spec, row_spec, gate_spec]
        out_shape = [sds((n, tn), F32), sds((n, tn), BF16), sds((n, tn), F32), sds((n, tn), F32),
                     sds((n, ngate * tn), BF16)]
    return pl.pallas_call(
        functools.partial(_in_kernel, prompt, q_scale), grid=(n // tm, nj),
        in_specs=in_specs, out_specs=out_specs, out_shape=out_shape,
        scratch_shapes=[pltpu.VMEM((tm, d), BF16)],
        compiler_params=_cparams(("parallel", "arbitrary")),
        name="in_proj")(x2, g, w_cat, b_cat, *tabs)


def _rope_tables(pos, rows):
    half = ROPE_DIM // 2
    inv_freq = 1.0 / (ROPE_THETA ** (jnp.arange(0, ROPE_DIM, 2, dtype=F32) / ROPE_DIM))
    ang = pos.astype(F32)[:, None] * inv_freq[None, :]
    cos, sin = jnp.cos(ang), jnp.sin(ang)
    t = pos.shape[0]
    sub = 64
    ones = jnp.ones((t, sub - ROPE_DIM), F32)
    zeros = jnp.zeros((t, sub - ROPE_DIM), F32)
    zh = jnp.zeros((t, half), F32)
    c = jnp.concatenate([cos, cos, ones], axis=1)
    s1 = jnp.concatenate([-sin, zh, zeros], axis=1)
    s2 = jnp.concatenate([zh, sin, zeros], axis=1)
    reps = (rows // t, LANES // sub)
    return tuple(jnp.tile(a, reps) for a in (c, s1, s2))


def _conv_kernel(conv_w, glu_ref, hist_ref, wdw_ref, bdw_ref, lng_ref, lnb_ref, wco_ref, bco_ref,
                 gate_ref, o_ref, xs_ref, acc_ref, c_ref):
    tt = glu_ref.shape[1]
    rows = HIST_PAD + tt
    i = pl.program_id(1)

    @pl.when(i == 0)
    def _():
        xs_ref[0, 0:HIST_PAD, :] = hist_ref[0]

    @pl.when(i > 0)
    def _():
        xs_ref[0, 0:HIST_PAD, :] = xs_ref[0, tt:tt + HIST_PAD, :]

    xs_ref[0, HIST_PAD:rows, :] = glu_ref[0]
    for b in range(1, SUBLANES):
        xs_ref[b, 0:rows - SUBLANES, :] = xs_ref[0, b:b + rows - SUBLANES, :]

    rc = 2 * SUBLANES
    off = HIST_PAD - (conv_w - 1)

    def chunk(r, carry):
        r0 = pl.multiple_of(r * rc, rc)
        accs = [None] * (rc // SUBLANES)
        for j in range(conv_w):
            a, b = divmod(off + j, SUBLANES)
            wj = wdw_ref[j]
            for u in range(len(accs)):
                start = pl.multiple_of(r0 + (a + u) * SUBLANES, SUBLANES)
                term = xs_ref[b, pl.ds(start, SUBLANES), :] * wj
                accs[u] = term if accs[u] is None else accs[u] + term
        acc_ref[pl.ds(r0, rc), :] = jnp.concatenate(accs, axis=0)
        return carry

    lax.fori_loop(0, tt // rc, chunk, 0)
    acc = acc_ref[...] + bdw_ref[...]
    mu = jnp.mean(acc, axis=-1, keepdims=True)
    xc = acc - mu
    var = jnp.mean(xc * xc, axis=-1, keepdims=True)
    y = xc * lax.rsqrt(var + EPS) * lng_ref[...] + lnb_ref[...]
    c_ref[...] = (y * jax.nn.sigmoid(y)).astype(BF16)
    out = jnp.dot(c_ref[...], wco_ref[...], preferred_element_type=F32) + bco_ref[...]
    o_ref[0] = (gate_ref[0].astype(F32) * out).astype(o_ref.dtype)


def _conv_branch(glu3, hist, w_dw, b_dw, ln_g, ln_b, w_co_bf, b_co, gates3, tt):
    b, t, c = glu3.shape
    d = w_co_bf.shape[1]
    conv_w = w_dw.shape[0]
    return pl.pallas_call(
        functools.partial(_conv_kernel, conv_w), grid=(b, t // tt),
        in_specs=[pl.BlockSpec((1, tt, c), lambda bi, i: (bi, i, 0)),
                  pl.BlockSpec((1, HIST_PAD, c), lambda bi, i: (bi, 0, 0)),
                  _full((conv_w, SUBLANES, c)), _full((1, c)), _full((1, c)), _full((1, c)),
                  _full((c, d)), _full((1, d)),
                  pl.BlockSpec((1, tt, d), lambda bi, i: (bi, i, 0))],
        out_specs=pl.BlockSpec((1, tt, d), lambda bi, i: (bi, i, 0)),
        out_shape=jax.ShapeDtypeStruct((b, t, d), BF16),
        scratch_shapes=[pltpu.VMEM((SUBLANES, HIST_PAD + tt, c), F32), pltpu.VMEM((tt, c), F32),
                        pltpu.VMEM((tt, c), BF16)],
        compiler_params=_cparams(("parallel", "arbitrary")),
        name="conv_branch")(glu3, hist, jnp.broadcast_to(w_dw[:, None, :], (conv_w, SUBLANES, c)), b_dw, ln_g, ln_b,
                            w_co_bf, b_co, gates3)


def _lambda(lam_ref, lam_init):
    lq1, lk1, lq2, lk2 = (lam_ref[r:r + 1, :] for r in range(4))
    return (jnp.exp(jnp.sum(lq1 * lk1, axis=-1, keepdims=True))
            - jnp.exp(jnp.sum(lq2 * lk2, axis=-1, keepdims=True)) + lam_init)


def _split_maps(q):
    lane = lax.broadcasted_iota(I32, q.shape, 1)
    zero = jnp.zeros_like(q)
    half = q.shape[1] // 2
    return jnp.where(lane < half, q, zero), jnp.where(lane >= half, q, zero)


def _nt(a, b):
    return lax.dot_general(a, b, (((1,), (1,)), ((), ())), preferred_element_type=F32)


def _attn_prompt_kernel(lam_init, q_ref, k_ref, vt_ref, lam_ref, sgt_ref, o_ref, acc_ref, sa_ref, sb_ref):
    tq = q_ref.shape[1]
    tk = vt_ref.shape[4]
    assert tq == 2 * tk
    qi = pl.program_id(2)
    qz = _split_maps(q_ref[0])
    acc_ref[...] = jnp.zeros_like(acc_ref)
    cdiff = (lax.broadcasted_iota(I32, (tk, tq), 0) // CHUNK) - (lax.broadcasted_iota(I32, (tk, tq), 1) // CHUNK)

    def scores(kb, s_ref):
        k0 = pl.multiple_of(kb * tk, tk)
        kblk = k_ref[0, pl.ds(k0, tk), :]
        for m in range(2):
            s_ref[m] = _nt(kblk, qz[m])

    def consume(kb, s_ref, carry, diag_block):
        vblk = vt_ref[0, 0, kb]
        new, alphas, ps = [], [], []
        for m in range(2):
            mx, l = carry[m]
            if diag_block is None:
                sm = s_ref[m]
            else:
                sm = jnp.where(cdiff <= -diag_block * (tk // CHUNK), s_ref[m], -jnp.inf)
            mn = jnp.maximum(mx, jnp.max(sm, axis=0, keepdims=True))
            alpha = jnp.exp2(mx - mn)
            p = jnp.exp2(sm - mn)
            new.append((mn, alpha * l + jnp.sum(p, axis=0, keepdims=True)))
            alphas.append(alpha)
            ps.append(p.astype(BF16))
        pv = [jnp.dot(vblk, ps[m], preferred_element_type=F32) for m in range(2)]
        for m in range(2):
            acc_ref[m] = alphas[m] * acc_ref[m] + pv[m]
        return tuple(new)

    def pair(j, carry):
        scores(2 * j + 1, sb_ref)
        carry = consume(2 * j, sa_ref, carry, None)
        scores(2 * j + 2, sa_ref)
        return consume(2 * j + 1, sb_ref, carry, None)

    init = tuple((jnp.full((1, tq), -jnp.inf, F32), jnp.zeros((1, tq), F32)) for _ in range(2))
    scores(0, sa_ref)
    carry = lax.fori_loop(0, qi, pair, init)
    scores(2 * qi + 1, sb_ref)
    carry = consume(2 * qi, sa_ref, carry, 0)
    carry = consume(2 * qi + 1, sb_ref, carry, 1)
    lam = _lambda(lam_ref, lam_init)
    ot = acc_ref[0] * (1.0 / carry[0][1]) - lam * (acc_ref[1] * (1.0 / carry[1][1]))
    ms = jnp.mean(ot * ot, axis=0, keepdims=True)
    yt = ot * lax.rsqrt(ms + EPS) * sgt_ref[...] * (1.0 - lam_init)
    o_ref[0] = yt.T.astype(o_ref.dtype)


def _attn_prompt(q3, k3, vt5, lam4, sgt, lam_init, tq):
    b, t, w = q3.shape
    n_heads = vt5.shape[1]
    hw = w // n_heads
    blk = lambda bi, h, i: (bi, i, h)
    return pl.pallas_call(
        functools.partial(_attn_prompt_kernel, lam_init), grid=(b, n_heads, t // tq),
        in_specs=[pl.BlockSpec((1, tq, hw), blk),
                  pl.BlockSpec((1, t, hw), lambda bi, h, i: (bi, 0, h)),
                  pl.BlockSpec((1, 1) + vt5.shape[2:], lambda bi, h, i: (bi, h, 0, 0, 0)),
                  _full(lam4.shape), _full(sgt.shape)],
        out_specs=pl.BlockSpec((1, tq, hw), blk),
        out_shape=jax.ShapeDtypeStruct((b, t, w), BF16),
        scratch_shapes=[pltpu.VMEM((2, vt5.shape[3], tq), F32), pltpu.VMEM((2, KV_BLOCK, tq), F32),
                        pltpu.VMEM((2, KV_BLOCK, tq), F32)],
        compiler_params=_cparams(("parallel", "parallel", "arbitrary")),
        name="attn_prompt")(q3, k3, vt5, lam4, sgt)


def _attn_sample_kernel(lam_init, n_heads, q_ref, ckt_ref, cv_ref, kn_ref, vn_ref, lam_ref, sg_ref, o_ref):
    past = ckt_ref.shape[2]
    hw = q_ref.shape[2] // n_heads
    lam = _lambda(lam_ref, lam_init)
    for h in range(n_heads):
        cols = slice(h * hw, (h + 1) * hw)
        qz = _split_maps(q_ref[0, :, cols])
        ckt = ckt_ref[0, cols, :].astype(BF16)
        cv = cv_ref[0, pl.ds(h, past, stride=n_heads), :].astype(BF16)
        kn = kn_ref[0, :, cols].astype(BF16)
        vn = vn_ref[0, :, cols].astype(BF16)
        outs = []
        for m in range(2):
            sp = jnp.dot(qz[m], ckt, preferred_element_type=F32)
            sn = _nt(qz[m], kn)
            mx = jnp.maximum(jnp.max(sp, axis=-1, keepdims=True), jnp.max(sn, axis=-1, keepdims=True))
            pp = jnp.exp2(sp - mx)
            pn = jnp.exp2(sn - mx)
            l = jnp.sum(pp, axis=-1, keepdims=True) + jnp.sum(pn, axis=-1, keepdims=True)
            acc = (jnp.dot(pp.astype(BF16), cv, preferred_element_type=F32)
                   + jnp.dot(pn.astype(BF16), vn, preferred_element_type=F32))
            outs.append(acc / l)
        o = outs[0] - lam * outs[1]
        y = _rms(o, sg_ref[...]) * (1.0 - lam_init)
        o_ref[0, :, cols] = y.astype(o_ref.dtype)


def _attn_sample(q3, ckt3, cv3, kn3, vn3, lam4, subln_g, lam_init, n_heads):
    b, t, w = q3.shape
    idx = lambda bi: (bi, 0, 0)
    return pl.pallas_call(
        functools.partial(_attn_sample_kernel, lam_init, n_heads), grid=(b,),
        in_specs=[pl.BlockSpec((1, t, w), idx), pl.BlockSpec((1,) + ckt3.shape[1:], idx),
                  pl.BlockSpec((1,) + cv3.shape[1:], idx), pl.BlockSpec((1, t, w), idx),
                  pl.BlockSpec((1, t, w), idx), _full(lam4.shape), _full(subln_g.shape)],
        out_specs=pl.BlockSpec((1, t, w), idx),
        out_shape=jax.ShapeDtypeStruct((b, t, w), BF16),
        compiler_params=_cparams(("parallel",)),
        name="attn_sample")(q3, ckt3, cv3, kn3, vn3, lam4, subln_g)


def _mid_kernel(n_groups, epg, x_ref, gc_ref, ga_ref, o_ref, wao_ref, wout_ref, g2_ref, wr2_ref, wrh_ref,
                br_ref, x1_ref, route_ref):
    attn = jnp.dot(o_ref[...], wao_ref[...], preferred_element_type=F32)
    merged = gc_ref[...].astype(F32) + ga_ref[...].astype(F32) * attn
    x1 = x_ref[...] + jnp.dot(merged.astype(BF16), wout_ref[...], preferred_element_type=F32)
    x1_ref[...] = x1
    xn2 = _rms(x1, g2_ref[...])

    n_exp = n_groups * epg
    hi = xn2.astype(BF16)
    lo = (xn2 - hi.astype(F32)).astype(BF16)
    both = jnp.dot(hi, wr2_ref[...], preferred_element_type=F32)
    lg = (both[:, :LANES] + both[:, LANES:]
          + jnp.dot(lo, wrh_ref[...], preferred_element_type=F32) + br_ref[...])
    lane = lax.broadcasted_iota(I32, lg.shape, 1)
    neg = -jnp.inf
    big = jnp.int32(2 ** 30)
    is_g = (lane >= n_exp) & (lane < n_exp + n_groups)
    gl = jnp.where(is_g, lg, neg)
    gmax = jnp.max(gl, axis=-1, keepdims=True)
    g_idx = jnp.min(jnp.where(gl == gmax, lane, big), axis=-1, keepdims=True) - n_exp
    p_g = 1.0 / jnp.sum(jnp.where(is_g, jnp.exp(gl - gmax), 0.0), axis=-1, keepdims=True)
    in_grp = (lane >= g_idx * epg) & (lane < (g_idx + 1) * epg)
    el = jnp.where(in_grp, lg, neg)
    v1 = jnp.max(el, axis=-1, keepdims=True)
    e1 = jnp.min(jnp.where(el == v1, lane, big), axis=-1, keepdims=True)
    el2 = jnp.where(lane == e1, neg, el)
    v2 = jnp.max(el2, axis=-1, keepdims=True)
    e2 = jnp.min(jnp.where(el2 == v2, lane, big), axis=-1, keepdims=True)
    t2 = jnp.exp(v2 - v1)
    den = 1.0 + t2
    w1 = p_g / den
    w2 = p_g * t2 / den
    route = jnp.where(lane == 0, e1.astype(F32),
                      jnp.where(lane == 1, e2.astype(F32),
                                jnp.where(lane == 2, w1, jnp.where(lane == 3, w2, 0.0))))
    route_ref[...] = route


def _mid(x2, gc2, gates2, o2, w_ao_bf, w_out_bf, g2, wr2, wrh, br, n_groups, epg, tm):
    n, d = x2.shape
    wo = o2.shape[1]
    row = lambda i: (i, 0)
    return pl.pallas_call(
        functools.partial(_mid_kernel, n_groups, epg), grid=(n // tm,),
        in_specs=[pl.BlockSpec((tm, d), row), pl.BlockSpec((tm, d), row),
                  pl.BlockSpec((tm, d), lambda i: (i, 1)), pl.BlockSpec((tm, wo), row),
                  _full(w_ao_bf.shape, True), _full(w_out_bf.shape, True), _full((1, d)),
                  _full(wr2.shape, True), _full(wrh.shape, True), _full(br.shape)],
        out_specs=[pl.BlockSpec((tm, d), row), pl.BlockSpec((tm, LANES), row)],
        out_shape=[jax.ShapeDtypeStruct((n, d), F32), jax.ShapeDtypeStruct((n, LANES), F32)],
        compiler_params=_cparams(("parallel",)),
        name="mid")(x2, gc2, gates2, o2, w_ao_bf, w_out_bf, g2, wr2, wrh, br)


def _dispatch_kernel(slot_ref, x1_ref, g2_ref, xs_in_hbm, xs_hbm, xn_ref, sem):
    del xs_in_hbm
    tmd = x1_ref.shape[0]
    xn_ref[...] = _rms(x1_ref[...], g2_ref[...])

    def row_copy(r, s):
        return pltpu.make_async_copy(xn_ref.at[pl.ds(r, 1)], xs_hbm.at[pl.ds(s, 1)], sem)

    for r in range(tmd):
        row_copy(r, slot_ref[0, 0, 2 * r]).start()
        row_copy(r, slot_ref[0, 0, 2 * r + 1]).start()

    for _ in range(2):
        pltpu.make_async_copy(xn_ref, xs_hbm.at[pl.ds(0, tmd)], sem).wait()


def _dispatch(slots, x1, g2, xs, tmd):
    n, d = x1.shape
    nt = n // tmd
    return pl.pallas_call(
        _dispatch_kernel, grid=(nt,),
        in_specs=[pl.BlockSpec((1, 1, 2 * tmd), lambda i: (i, 0, 0), memory_space=pltpu.SMEM),
                  pl.BlockSpec((tmd, d), lambda i: (i, 0)), _full((1, d)),
                  pl.BlockSpec(memory_space=pl.ANY)],
        out_specs=pl.BlockSpec(memory_space=pl.ANY),
        out_shape=jax.ShapeDtypeStruct(xs.shape, xs.dtype),
        scratch_shapes=[pltpu.VMEM((tmd, d), F32), pltpu.SemaphoreType.DMA],
        input_output_aliases={3: 0},
        compiler_params=_cparams(("arbitrary",)),
        name="dispatch")(slots.reshape(nt, 1, 2 * tmd), x1, g2, xs)


def _expert_kernel(te_ref, nt_ref, xs_ref, wg_ref, wu_ref, wd_ref, ys_ref, wgb_ref, wub_ref, wdb_ref):
    i = pl.program_id(0)
    used = i < nt_ref[0]
    new_expert = (i == 0) | (te_ref[i] != te_ref[jnp.maximum(i - 1, 0)])

    @pl.when(used & new_expert)
    def _():
        wgb_ref[...] = wg_ref[0].astype(BF16)
        wub_ref[...] = wu_ref[0].astype(BF16)
        wdb_ref[...] = wd_ref[0].astype(BF16)

    @pl.when(used)
    def _():
        x = xs_ref[...].astype(BF16)
        hg = jnp.dot(x, wgb_ref[...], preferred_element_type=F32)
        hu = jnp.dot(x, wub_ref[...], preferred_element_type=F32)
        h = (hg * jax.nn.sigmoid(hg)) * hu
        ys_ref[...] = jnp.dot(h.astype(BF16), wdb_ref[...], preferred_element_type=F32)

    @pl.when(jnp.logical_not(used))
    def _():
        ys_ref[...] = jnp.zeros_like(ys_ref)


def _experts(tile_expert, n_tiles, xs, wg, wu, wd, tme):
    s, d = xs.shape
    de = wg.shape[2]
    tile = lambda i, te, nt: (jnp.minimum(i, nt[0] - 1), 0)
    wmap = lambda i, te, nt: (te[jnp.minimum(i, nt[0] - 1)], 0, 0)
    grid_spec = pltpu.PrefetchScalarGridSpec(
        num_scalar_prefetch=2, grid=(s // tme,),
        in_specs=[pl.BlockSpec((tme, d), tile),
                  pl.BlockSpec((1, d, de), wmap), pl.BlockSpec((1, d, de), wmap), pl.BlockSpec((1, de, d), wmap)],
        out_specs=pl.BlockSpec((tme, d), lambda i, te, nt: (i, 0)),
        scratch_shapes=[pltpu.VMEM((d, de), BF16), pltpu.VMEM((d, de), BF16), pltpu.VMEM((de, d), BF16)])
    return pl.pallas_call(
        _expert_kernel, grid_spec=grid_spec,
        out_shape=jax.ShapeDtypeStruct((s, d), F32),
        compiler_params=_cparams(("arbitrary",)),
        name="experts")(tile_expert, n_tiles, xs, wg, wu, wd)


def _combine_kernel(slot_ref, x1_ref, route_ref, gf_ref, ys_hbm, y_ref, buf_ref, sem):
    tmf = x1_ref.shape[0]

    def row_copy(s, k, r):
        return pltpu.make_async_copy(ys_hbm.at[pl.ds(s, 1)], buf_ref.at[k, pl.ds(r, 1)], sem)

    for r in range(tmf):
        row_copy(slot_ref[0, 0, 2 * r], 0, r).start()
        row_copy(slot_ref[0, 0, 2 * r + 1], 1, r).start()

    for k in range(2):
        pltpu.make_async_copy(ys_hbm.at[pl.ds(0, tmf)], buf_ref.at[k], sem).wait()
    w1 = route_ref[:, 2:3]
    w2 = route_ref[:, 3:4]
    x2 = x1_ref[...] + (w1 * buf_ref[0] + w2 * buf_ref[1])
    y_ref[...] = _rms(x2, gf_ref[...])


def _combine(slots, x1, route, gf, ys, tmf):
    n, d = x1.shape
    nt = n // tmf
    row = lambda i: (i, 0)
    return pl.pallas_call(
        _combine_kernel, grid=(nt,),
        in_specs=[pl.BlockSpec((1, 1, 2 * tmf), lambda i: (i, 0, 0), memory_space=pltpu.SMEM),
                  pl.BlockSpec((tmf, d), row), pl.BlockSpec((tmf, LANES), row), _full((1, d)),
                  pl.BlockSpec(memory_space=pl.ANY)],
        out_specs=pl.BlockSpec((tmf, d), row),
        out_shape=jax.ShapeDtypeStruct((n, d), F32),
        scratch_shapes=[pltpu.VMEM((2, tmf, d), F32), pltpu.SemaphoreType.DMA],
        compiler_params=_cparams(("arbitrary",)),
        name="combine")(slots.reshape(nt, 1, 2 * tmf), x1, route, gf, ys)


def _pick(n, pref):
    t = min(n, pref)
    while n % t:
        t //= 2
    return t


def _mixer(x3, hist, k_past_t, v_past, pos, p, n_heads, lam_init):
    b, t, d = x3.shape
    n = b * t
    x2 = x3.reshape(n, d)
    c_conv = p["w_dw"].shape[1]
    prompt = k_past_t is None
    tm = _pick(n, 512)
    tabs = _rope_tables(pos, max(t, tm))
    q_scale = (d // n_heads // 4) ** -0.5 * LOG2E
    outs = _in_proj(x2, p["norm1_g"], p["w_cat"], p["b_cat"], tabs, q_scale, tm, t, prompt)
    if prompt:
        glu, q, k_bf, k_t, v, v_t, gates = outs
    else:
        glu, q, k, v, gates = outs

    glu3 = glu.reshape(b, t, c_conv)
    gates3 = gates.reshape(b, t, 2 * d)
    gc = _conv_branch(glu3, hist, p["w_dw"], p["b_dw"], p["ln_g"], p["ln_b"], p["w_co"], p["b_co"],
                      gates3, _pick(t, 256))

    w = q.shape[1]
    q3 = q.reshape(b, t, w)
    hd = w // n_heads // 2
    if prompt:
        o = _attn_prompt(q3, k_bf.reshape(b, t, w), v_t, p["lam4"], p["subln_g"].reshape(-1, 1), lam_init,
                         2 * KV_BLOCK)
        k_out = k_t.reshape(b, n_heads, 2, hd, t).transpose(0, 4, 1, 2, 3)[None]
    else:
        k3 = k.reshape(b, t, w)
        o = _attn_sample(q3, k_past_t, v_past, k3, v.reshape(b, t, w), p["lam4"], p["subln_g"], lam_init, n_heads)
        k_out = k3.reshape(1, b, t, n_heads, 2, hd)
    v_out = v.reshape(1, b, t, n_heads, w // n_heads)

    x1, route = _mid(x2, gc.reshape(n, d), gates, o.reshape(n, w), p["w_ao"], p["w_out"],
                     p["norm2_g"], p["wr2"], p["wrh"], p["br"], p["n_groups"], p["epg"], _pick(n, 512))
    new_hist = glu3[:, t - (p["w_dw"].shape[0] - 1):, :]
    return x1, route, k_out, v_out, new_hist[None]


def kernel(x_prompt, x_sample, cache_k, cache_v, state_conv, norm1_g, w_in, b_glu, w_dw, b_dw, conv_ln_g, conv_ln_b, w_conv_out, b_conv_out, lambda_q1, lambda_k1, lambda_q2, lambda_k2, subln_g, w_attn_out, w_out, norm2_g, w_router_group, b_router_group, w_router_expert, b_router_expert, w_exp_gate, w_exp_up, w_exp_down, final_norm_g):
    bp, tp, d = x_prompt.shape
    bs, ts, _ = x_sample.shape
    assert w_in.shape[0] == 1
    l = 0
    past = cache_k.shape[2]
    n_heads = cache_k.shape[3]
    hd = cache_k.shape[5]
    v_dim = cache_v.shape[4]
    conv_w = w_dw.shape[1]
    c_conv = w_dw.shape[2]
    n_groups = w_router_group.shape[2]
    n_exp = w_router_expert.shape[2]
    epg = n_exp // n_groups
    lam_init = 0.8 - 0.6 * math.exp(-0.3 * l)
    assert c_conv == 1024 and n_heads * 2 * hd == 1024 and n_heads * v_dim == 1024 and tp % (2 * KV_BLOCK) == 0

    hc = c_conv // 2
    wl = w_in[l]
    w_cat = jnp.concatenate([wl[:, :hc], wl[:, c_conv:c_conv + hc], wl[:, hc:c_conv], wl[:, c_conv + hc:]],
                            axis=1).astype(BF16)
    bg = b_glu[l]
    b_cat = jnp.concatenate([bg[:hc], bg[c_conv:c_conv + hc], bg[hc:c_conv], bg[c_conv + hc:]])[None]

    wr = jnp.zeros((d, LANES), F32)
    wr = wr.at[:, :n_exp].set(w_router_expert[l]).at[:, n_exp:n_exp + n_groups].set(w_router_group[l])
    wrh = wr.astype(BF16)
    wrl = (wr - wrh.astype(F32)).astype(BF16)
    br = jnp.zeros((1, LANES), F32)
    br = br.at[0, :n_exp].set(b_router_expert[l]).at[0, n_exp:n_exp + n_groups].set(b_router_group[l])
    p = dict(
        norm1_g=norm1_g[l][None], w_cat=w_cat, b_cat=b_cat,
        w_dw=w_dw[l], b_dw=b_dw[l][None], ln_g=conv_ln_g[l][None], ln_b=conv_ln_b[l][None],
        w_co=w_conv_out[l].astype(BF16), b_co=b_conv_out[l][None],
        lam4=jnp.stack([lambda_q1[l], lambda_k1[l], lambda_q2[l], lambda_k2[l]]),
        subln_g=subln_g[l][None], w_ao=w_attn_out[l].astype(BF16), w_out=w_out[l].astype(BF16),
        norm2_g=norm2_g[l][None], wr2=jnp.concatenate([wrh, wrl], axis=1), wrh=wrh, br=br,
        n_groups=n_groups, epg=epg)

    hist_p = jnp.zeros((bp, HIST_PAD, c_conv), F32)
    hist_s = jnp.concatenate(
        [jnp.zeros((bs, HIST_PAD - (conv_w - 1), c_conv), F32), state_conv[l]], axis=1)
    ckt = cache_k[l].transpose(0, 2, 3, 4, 1).reshape(bs, n_heads * 2 * hd, past)
    cv = cache_v[l].reshape(bs, past * n_heads, v_dim)

    x1p, routep, kp, vp, cp = _mixer(x_prompt, hist_p, None, None, jnp.arange(tp), p, n_heads, lam_init)
    x1s, routes, ks, vs, cs = _mixer(x_sample, hist_s, ckt, cv, past + jnp.arange(ts), p, n_heads, lam_init)

    np_, ns_ = bp * tp, bs * ts
    n = np_ + ns_
    tme = 512
    e_idx = jnp.concatenate([routep[:, :2], routes[:, :2]], axis=0).astype(I32)
    onehot = (e_idx[:, :, None] == jnp.arange(n_exp, dtype=I32)[None, None, :]).astype(I32)
    per_tok = onehot.sum(axis=1)
    before = jnp.cumsum(per_tok, axis=0) - per_tok
    rank = jnp.sum(onehot * before[:, None, :], axis=-1)
    counts = per_tok.sum(axis=0)
    tiles_per = (counts + tme - 1) // tme
    tile_end = jnp.cumsum(tiles_per)
    tile_start = tile_end - tiles_per
    slots = (jnp.sum(onehot * (tile_start * tme)[None, None, :], axis=-1) + rank).astype(I32)
    max_tiles = (2 * n) // tme + n_exp
    tile_ids = jnp.arange(max_tiles, dtype=I32)
    tile_expert = jnp.minimum(jnp.sum(tile_ids[:, None] >= tile_end[None, :], axis=1), n_exp - 1).astype(I32)
    n_tiles = tile_end[-1:].astype(I32)

    g2 = norm2_g[l][None]
    xs = jnp.zeros((max_tiles * tme, d), F32)
    xs = _dispatch(slots[:np_].reshape(-1), x1p, g2, xs, _pick(np_, 256))
    xs = _dispatch(slots[np_:].reshape(-1), x1s, g2, xs, _pick(ns_, 256))
    ys = _experts(tile_expert, n_tiles, xs, w_exp_gate[l], w_exp_up[l], w_exp_down[l], tme)
    gf = final_norm_g[None]
    yp = _combine(slots[:np_].reshape(-1), x1p, routep, gf, ys, _pick(np_, 256))
    ysm = _combine(slots[np_:].reshape(-1), x1s, routes, gf, ys, _pick(ns_, 256))

    return (yp.reshape(bp, tp, d), ysm.reshape(bs, ts, d), kp, vp, cp, ks, vs, cs)
```

```python
import functools
import math

import jax
import jax.numpy as jnp
from jax import lax
from jax.experimental import pallas as pl
from jax.experimental.pallas import tpu as pltpu

F32 = jnp.float32
BF16 = jnp.bfloat16
I32 = jnp.int32

EPS = 1e-6
CHUNK = 64
ROPE_DIM = 16
ROPE_THETA = 500000.0
LOG2E = 1.4426950408889634
LANES = 128
SUBLANES = 8
KV_BLOCK = 256
HIST_PAD = 32
VMEM_LIMIT = 56 * 1024 * 1024


def _cparams(sem):
    return pltpu.CompilerParams(dimension_semantics=sem, vmem_limit_bytes=VMEM_LIMIT)


def _full(shape, single=False):
    idx = lambda *_: (0,) * len(shape)
    if single:
        return pl.BlockSpec(shape, idx, pipeline_mode=pl.Buffered(1))
    return pl.BlockSpec(shape, idx)


def _rms(x, g):
    return x * lax.rsqrt(jnp.mean(x * x, axis=-1, keepdims=True) + EPS) * g


def _rms_rows(x_ref, g_ref, xn_ref):
    tm = x_ref.shape[0]
    rc = min(tm, 128)

    def body(r, carry):
        r0 = pl.multiple_of(r * rc, rc)
        xn_ref[pl.ds(r0, rc), :] = _rms(x_ref[pl.ds(r0, rc), :], g_ref[...]).astype(xn_ref.dtype)
        return carry

    lax.fori_loop(0, tm // rc, body, 0)


def _rope(z, c_ref, s1_ref, s2_ref):
    outs = []
    for c in range(z.shape[1] // LANES):
        zc = z[:, c * LANES:(c + 1) * LANES]
        up = pltpu.roll(zc, LANES - ROPE_DIM // 2, axis=1)
        dn = pltpu.roll(zc, ROPE_DIM // 2, axis=1)
        outs.append(zc * c_ref[...] + up * s1_ref[...] + dn * s2_ref[...])
    return jnp.concatenate(outs, axis=1)


def _in_kernel(prompt, q_scale, x_ref, g_ref, w_ref, b_ref, c_ref, s1_ref, s2_ref, *rest):
    if prompt:
        glu_ref, q_ref, k_ref, kt_ref, v_ref, vt_ref, gate_ref, xn_ref = rest
    else:
        glu_ref, q_ref, k_ref, v_ref, gate_ref, xn_ref = rest
    j = pl.program_id(1)
    xn_ref = xn_ref.at[pl.program_id(2)]
    tm, tn = x_ref.shape[0], w_ref.shape[1]
    cw = 2 * LANES
    nchunk = tn // cw

    @pl.when(j == 0)
    def _():
        _rms_rows(x_ref, g_ref, xn_ref)

    def zchunk(c0):
        return jnp.dot(xn_ref[...], w_ref[:, c0:c0 + cw], preferred_element_type=F32)

    @pl.when(j < 2)
    def _():
        half = tn // 2
        for c in range(half // cw):
            a = zchunk(c * cw) + b_ref[:, c * cw:(c + 1) * cw]
            b = zchunk(half + c * cw) + b_ref[:, half + c * cw:half + (c + 1) * cw]
            glu_ref[:, c * cw:(c + 1) * cw] = a * jax.nn.sigmoid(b)

    @pl.when(j == 2)
    def _():
        for c in range(nchunk):
            zr = _rope(zchunk(c * cw), c_ref, s1_ref, s2_ref)
            q_ref[:, c * cw:(c + 1) * cw] = (zr * q_scale).astype(q_ref.dtype)

    @pl.when(j == 3)
    def _():
        for c in range(nchunk):
            zr = _rope(zchunk(c * cw), c_ref, s1_ref, s2_ref)
            k_ref[:, c * cw:(c + 1) * cw] = zr.astype(k_ref.dtype)
            if prompt:
                kt_ref[0, c * cw:(c + 1) * cw, :] = zr.T

    @pl.when(j == 4)
    def _():
        for c in range(nchunk):
            z = zchunk(c * cw)
            v_ref[:, c * cw:(c + 1) * cw] = z
            if prompt:
                kb = vt_ref.shape[4]
                for hh in range(cw // LANES):
                    for blk in range(tm // kb):
                        vt_ref[0, c * (cw // LANES) + hh, blk] = (
                            z[blk * kb:(blk + 1) * kb, hh * LANES:(hh + 1) * LANES].T.astype(BF16))

    @pl.when(j >= 5)
    def _():
        for c in range(nchunk):
            gate_ref[:, c * cw:(c + 1) * cw] = jax.nn.sigmoid(zchunk(c * cw)).astype(gate_ref.dtype)


def _in_proj(x2, g, w_cat, b_cat, tabs, q_scale, tm, seq, prompt):
    n, d = x2.shape
    tn = 1024
    nj = w_cat.shape[1] // tn
    ngate = nj - 5
    nper = tabs[0].shape[0] // tm
    grp = 2 if (n // tm) % 2 == 0 else 1

    def rows_at(j_lo, j_hi):
        def row(io, j, ii):
            return io * grp + jnp.where(j < j_lo, 0, jnp.where(j <= j_hi, ii, grp - 1))
        return row

    x_row = rows_at(0, 0)
    glu_row, rope_row = rows_at(0, 1), rows_at(2, 3)
    q_row, k_row, v_row, gate_row = rows_at(2, 2), rows_at(3, 3), rows_at(4, 4), rows_at(5, nj - 1)
    in_specs = [pl.BlockSpec((tm, d), lambda io, j, ii: (x_row(io, j, ii), 0)), _full((1, d)),
                pl.BlockSpec((d, tn), lambda io, j, ii: (0, j)),
                pl.BlockSpec((1, tn), lambda io, j, ii: (0, jnp.minimum(j, 1)))]
    in_specs += [pl.BlockSpec((tm, LANES), lambda io, j, ii: (rope_row(io, j, ii) % nper, 0))] * 3
    glu_spec = pl.BlockSpec((tm, tn // 2), lambda io, j, ii: (glu_row(io, j, ii), jnp.minimum(j, 1)))
    gate_spec = pl.BlockSpec((tm, tn), lambda io, j, ii: (gate_row(io, j, ii), jnp.clip(j - 5, 0, ngate - 1)))

    def row_spec(row):
        return pl.BlockSpec((tm, tn), lambda io, j, ii: (row(io, j, ii), 0))

    sds = jax.ShapeDtypeStruct
    if prompt:
        tpb = seq // tm
        nb = n // seq

        def t_spec(shape, row):
            def idx(io, j, ii):
                r = row(io, j, ii)
                return (r // tpb, 0, r % tpb) + (0,) * (len(shape) - 3)
            return pl.BlockSpec(shape, idx)

        out_specs = [glu_spec, row_spec(q_row), row_spec(k_row), t_spec((1, tn, tm), k_row), row_spec(v_row),
                     t_spec((1, tn // LANES, tm // KV_BLOCK, LANES, KV_BLOCK), v_row), gate_spec]
        out_shape = [sds((n, tn), F32), sds((n, tn), BF16), sds((n, tn), BF16), sds((nb, tn, seq), F32),
                     sds((n, tn), F32), sds((nb, tn // LANES, seq // KV_BLOCK, LANES, KV_BLOCK), BF16),
                     sds((n, ngate * tn), BF16)]
    else:
        out_specs = [glu_spec, row_spec(q_row), row_spec(k_row), row_spec(v_row), gate_spec]
        out_shape = [sds((n, tn), F32), sds((n, tn), BF16), sds((n, tn), F32), sds((n, tn), F32),
                     sds((n, ngate * tn), BF16)]
    return pl.pallas_call(
        functools.partial(_in_kernel, prompt, q_scale), grid=(n // (tm * grp), nj, grp),
        in_specs=in_specs, out_specs=out_specs, out_shape=out_shape,
        scratch_shapes=[pltpu.VMEM((grp, tm, d), BF16)],
        compiler_params=_cparams(("parallel", "arbitrary", "arbitrary")),
        name="in_proj")(x2, g, w_cat, b_cat, *tabs)


def _rope_tables(pos, rows):
    half = ROPE_DIM // 2
    inv_freq = 1.0 / (ROPE_THETA ** (jnp.arange(0, ROPE_DIM, 2, dtype=F32) / ROPE_DIM))
    ang = pos.astype(F32)[:, None] * inv_freq[None, :]
    cos, sin = jnp.cos(ang), jnp.sin(ang)
    t = pos.shape[0]
    sub = 64
    ones = jnp.ones((t, sub - ROPE_DIM), F32)
    zeros = jnp.zeros((t, sub - ROPE_DIM), F32)
    zh = jnp.zeros((t, half), F32)
    c = jnp.concatenate([cos, cos, ones], axis=1)
    s1 = jnp.concatenate([-sin, zh, zeros], axis=1)
    s2 = jnp.concatenate([zh, sin, zeros], axis=1)
    reps = (rows // t, LANES // sub)
    return tuple(jnp.tile(a, reps) for a in (c, s1, s2))


def _conv_kernel(conv_w, glu_ref, hist_ref, wdw_ref, bdw_ref, lng_ref, lnb_ref, wco_ref, bco_ref,
                 gate_ref, o_ref, xs_ref, acc_ref, c_ref):
    tt = glu_ref.shape[1]
    rows = HIST_PAD + tt
    i = pl.program_id(1)

    @pl.when(i == 0)
    def _():
        xs_ref[0, 0:HIST_PAD, :] = hist_ref[0]

    @pl.when(i > 0)
    def _():
        xs_ref[0, 0:HIST_PAD, :] = xs_ref[0, tt:tt + HIST_PAD, :]

    xs_ref[0, HIST_PAD:rows, :] = glu_ref[0]
    for b in range(1, SUBLANES):
        xs_ref[b, 0:rows - SUBLANES, :] = xs_ref[0, b:b + rows - SUBLANES, :]

    rc = 2 * SUBLANES
    off = HIST_PAD - (conv_w - 1)

    def chunk(r, carry):
        r0 = pl.multiple_of(r * rc, rc)
        accs = [None] * (rc // SUBLANES)
        for j in range(conv_w):
            a, b = divmod(off + j, SUBLANES)
            wj = wdw_ref[j]
            for u in range(len(accs)):
                start = pl.multiple_of(r0 + (a + u) * SUBLANES, SUBLANES)
                term = xs_ref[b, pl.ds(start, SUBLANES), :] * wj
                accs[u] = term if accs[u] is None else accs[u] + term
        acc_ref[pl.ds(r0, rc), :] = jnp.concatenate(accs, axis=0)
        return carry

    lax.fori_loop(0, tt // rc, chunk, 0)
    acc = acc_ref[...] + bdw_ref[...]
    mu = jnp.mean(acc, axis=-1, keepdims=True)
    xc = acc - mu
    var = jnp.mean(xc * xc, axis=-1, keepdims=True)
    y = xc * lax.rsqrt(var + EPS) * lng_ref[...] + lnb_ref[...]
    c_ref[...] = (y * jax.nn.sigmoid(y)).astype(BF16)
    out = jnp.dot(c_ref[...], wco_ref[...], preferred_element_type=F32) + bco_ref[...]
    o_ref[0] = (gate_ref[0].astype(F32) * out).astype(o_ref.dtype)


def _conv_branch(glu3, hist, w_dw, b_dw, ln_g, ln_b, w_co_bf, b_co, gates3, tt):
    b, t, c = glu3.shape
    d = w_co_bf.shape[1]
    conv_w = w_dw.shape[0]
    return pl.pallas_call(
        functools.partial(_conv_kernel, conv_w), grid=(b, t // tt),
        in_specs=[pl.BlockSpec((1, tt, c), lambda bi, i: (bi, i, 0)),
                  pl.BlockSpec((1, HIST_PAD, c), lambda bi, i: (bi, 0, 0)),
                  _full((conv_w, SUBLANES, c)), _full((1, c)), _full((1, c)), _full((1, c)),
                  _full((c, d)), _full((1, d)),
                  pl.BlockSpec((1, tt, d), lambda bi, i: (bi, i, 0))],
        out_specs=pl.BlockSpec((1, tt, d), lambda bi, i: (bi, i, 0)),
        out_shape=jax.ShapeDtypeStruct((b, t, d), BF16),
        scratch_shapes=[pltpu.VMEM((SUBLANES, HIST_PAD + tt, c), F32), pltpu.VMEM((tt, c), F32),
                        pltpu.VMEM((tt, c), BF16)],
        compiler_params=_cparams(("parallel", "arbitrary")),
        name="conv_branch")(glu3, hist, jnp.broadcast_to(w_dw[:, None, :], (conv_w, SUBLANES, c)), b_dw, ln_g, ln_b,
                            w_co_bf, b_co, gates3)


def _lambda(lam_ref, lam_init):
    lq1, lk1, lq2, lk2 = (lam_ref[r:r + 1, :] for r in range(4))
    return (jnp.exp(jnp.sum(lq1 * lk1, axis=-1, keepdims=True))
            - jnp.exp(jnp.sum(lq2 * lk2, axis=-1, keepdims=True)) + lam_init)


def _split_maps(q):
    lane = lax.broadcasted_iota(I32, q.shape, 1)
    zero = jnp.zeros_like(q)
    half = q.shape[1] // 2
    return jnp.where(lane < half, q, zero), jnp.where(lane >= half, q, zero)


def _nt(a, b):
    return lax.dot_general(a, b, (((1,), (1,)), ((), ())), preferred_element_type=F32)


def _attn_prompt_kernel(lam_init, q_ref, k_ref, vt_ref, lam_ref, sgt_ref, o_ref, acc_ref, sa_ref, sb_ref):
    tq = q_ref.shape[1]
    tk = vt_ref.shape[4]
    assert tq == 2 * tk
    qi = pl.program_id(2)
    qz = _split_maps(q_ref[0])
    acc_ref[...] = jnp.zeros_like(acc_ref)
    cdiff = (lax.broadcasted_iota(I32, (tk, tq), 0) // CHUNK) - (lax.broadcasted_iota(I32, (tk, tq), 1) // CHUNK)

    def scores(kb, s_ref):
        k0 = pl.multiple_of(kb * tk, tk)
        kblk = k_ref[0, pl.ds(k0, tk), :]
        for m in range(2):
            s_ref[m] = _nt(kblk, qz[m])

    def consume(kb, s_ref, carry, diag_block):
        vblk = vt_ref[0, 0, kb]
        new, alphas, ps = [], [], []
        for m in range(2):
            mx, l = carry[m]
            if diag_block is None:
                sm = s_ref[m]
            else:
                sm = jnp.where(cdiff <= -diag_block * (tk // CHUNK), s_ref[m], -jnp.inf)
            mn = jnp.maximum(mx, jnp.max(sm, axis=0, keepdims=True))
            alpha = jnp.exp2(mx - mn)
            p = jnp.exp2(sm - mn)
            new.append((mn, alpha * l + jnp.sum(p, axis=0, keepdims=True)))
            alphas.append(alpha)
            ps.append(p.astype(BF16))
        pv = [jnp.dot(vblk, ps[m], preferred_element_type=F32) for m in range(2)]
        for m in range(2):
            acc_ref[m] = alphas[m] * acc_ref[m] + pv[m]
        return tuple(new)

    def pair(j, carry):
        scores(2 * j + 1, sb_ref)
        carry = consume(2 * j, sa_ref, carry, None)
        scores(2 * j + 2, sa_ref)
        return consume(2 * j + 1, sb_ref, carry, None)

    init = tuple((jnp.full((1, tq), -jnp.inf, F32), jnp.zeros((1, tq), F32)) for _ in range(2))
    scores(0, sa_ref)
    carry = lax.fori_loop(0, qi, pair, init)
    scores(2 * qi + 1, sb_ref)
    carry = consume(2 * qi, sa_ref, carry, 0)
    carry = consume(2 * qi + 1, sb_ref, carry, 1)
    lam = _lambda(lam_ref, lam_init)
    ot = acc_ref[0] * (1.0 / carry[0][1]) - lam * (acc_ref[1] * (1.0 / carry[1][1]))
    ms = jnp.mean(ot * ot, axis=0, keepdims=True)
    yt = ot * lax.rsqrt(ms + EPS) * sgt_ref[...] * (1.0 - lam_init)
    o_ref[0] = yt.T.astype(o_ref.dtype)


def _attn_prompt(q3, k3, vt5, lam4, sgt, lam_init, tq):
    b, t, w = q3.shape
    n_heads = vt5.shape[1]
    hw = w // n_heads
    blk = lambda bi, h, i: (bi, i, h)
    return pl.pallas_call(
        functools.partial(_attn_prompt_kernel, lam_init), grid=(b, n_heads, t // tq),
        in_specs=[pl.BlockSpec((1, tq, hw), blk),
                  pl.BlockSpec((1, t, hw), lambda bi, h, i: (bi, 0, h)),
                  pl.BlockSpec((1, 1) + vt5.shape[2:], lambda bi, h, i: (bi, h, 0, 0, 0)),
                  _full(lam4.shape), _full(sgt.shape)],
        out_specs=pl.BlockSpec((1, tq, hw), blk),
        out_shape=jax.ShapeDtypeStruct((b, t, w), BF16),
        scratch_shapes=[pltpu.VMEM((2, vt5.shape[3], tq), F32), pltpu.VMEM((2, KV_BLOCK, tq), F32),
                        pltpu.VMEM((2, KV_BLOCK, tq), F32)],
        compiler_params=_cparams(("parallel", "parallel", "arbitrary")),
        name="attn_prompt")(q3, k3, vt5, lam4, sgt)


def _attn_sample_kernel(lam_init, n_heads, q_ref, ckt_ref, cv_ref, kn_ref, vn_ref, lam_ref, sg_ref, o_ref):
    past = ckt_ref.shape[2]
    hw = q_ref.shape[2] // n_heads
    lam = _lambda(lam_ref, lam_init)
    for h in range(n_heads):
        cols = slice(h * hw, (h + 1) * hw)
        qz = _split_maps(q_ref[0, :, cols])
        ckt = ckt_ref[0, cols, :].astype(BF16)
        cv = cv_ref[0, pl.ds(h, past, stride=n_heads), :].astype(BF16)
        kn = kn_ref[0, :, cols].astype(BF16)
        vn = vn_ref[0, :, cols].astype(BF16)
        outs = []
        for m in range(2):
            sp = jnp.dot(qz[m], ckt, preferred_element_type=F32)
            sn = _nt(qz[m], kn)
            mx = jnp.maximum(jnp.max(sp, axis=-1, keepdims=True), jnp.max(sn, axis=-1, keepdims=True))
            pp = jnp.exp2(sp - mx)
            pn = jnp.exp2(sn - mx)
            l = jnp.sum(pp, axis=-1, keepdims=True) + jnp.sum(pn, axis=-1, keepdims=True)
            acc = (jnp.dot(pp.astype(BF16), cv, preferred_element_type=F32)
                   + jnp.dot(pn.astype(BF16), vn, preferred_element_type=F32))
            outs.append(acc / l)
        o = outs[0] - lam * outs[1]
        y = _rms(o, sg_ref[...]) * (1.0 - lam_init)
        o_ref[0, :, cols] = y.astype(o_ref.dtype)


def _attn_sample(q3, ckt3, cv3, kn3, vn3, lam4, subln_g, lam_init, n_heads):
    b, t, w = q3.shape
    idx = lambda bi: (bi, 0, 0)
    return pl.pallas_call(
        functools.partial(_attn_sample_kernel, lam_init, n_heads), grid=(b,),
        in_specs=[pl.BlockSpec((1, t, w), idx), pl.BlockSpec((1,) + ckt3.shape[1:], idx),
                  pl.BlockSpec((1,) + cv3.shape[1:], idx), pl.BlockSpec((1, t, w), idx),
                  pl.BlockSpec((1, t, w), idx), _full(lam4.shape), _full(subln_g.shape)],
        out_specs=pl.BlockSpec((1, t, w), idx),
        out_shape=jax.ShapeDtypeStruct((b, t, w), BF16),
        compiler_params=_cparams(("parallel",)),
        name="attn_sample")(q3, ckt3, cv3, kn3, vn3, lam4, subln_g)


def _mid_kernel(n_groups, epg, x_ref, gc_ref, ga_ref, o_ref, wao_ref, wout_ref, g2_ref, wr2_ref, wrh_ref,
                br_ref, x1_ref, route_ref):
    attn = jnp.dot(o_ref[...], wao_ref[...], preferred_element_type=F32)
    merged = gc_ref[...].astype(F32) + ga_ref[...].astype(F32) * attn
    x1 = x_ref[...] + jnp.dot(merged.astype(BF16), wout_ref[...], preferred_element_type=F32)
    x1_ref[...] = x1
    xn2 = _rms(x1, g2_ref[...])

    n_exp = n_groups * epg
    hi = xn2.astype(BF16)
    lo = (xn2 - hi.astype(F32)).astype(BF16)
    both = jnp.dot(hi, wr2_ref[...], preferred_element_type=F32)
    lg = (both[:, :LANES] + both[:, LANES:]
          + jnp.dot(lo, wrh_ref[...], preferred_element_type=F32) + br_ref[...])
    lane = lax.broadcasted_iota(I32, lg.shape, 1)
    neg = -jnp.inf
    big = jnp.int32(2 ** 30)
    is_g = (lane >= n_exp) & (lane < n_exp + n_groups)
    gl = jnp.where(is_g, lg, neg)
    gmax = jnp.max(gl, axis=-1, keepdims=True)
    g_idx = jnp.min(jnp.where(gl == gmax, lane, big), axis=-1, keepdims=True) - n_exp
    p_g = 1.0 / jnp.sum(jnp.where(is_g, jnp.exp(gl - gmax), 0.0), axis=-1, keepdims=True)
    in_grp = (lane >= g_idx * epg) & (lane < (g_idx + 1) * epg)
    el = jnp.where(in_grp, lg, neg)
    v1 = jnp.max(el, axis=-1, keepdims=True)
    e1 = jnp.min(jnp.where(el == v1, lane, big), axis=-1, keepdims=True)
    el2 = jnp.where(lane == e1, neg, el)
    v2 = jnp.max(el2, axis=-1, keepdims=True)
    e2 = jnp.min(jnp.where(el2 == v2, lane, big), axis=-1, keepdims=True)
    t2 = jnp.exp(v2 - v1)
    den = 1.0 + t2
    w1 = p_g / den
    w2 = p_g * t2 / den
    route = jnp.where(lane == 0, e1.astype(F32),
                      jnp.where(lane == 1, e2.astype(F32),
                                jnp.where(lane == 2, w1, jnp.where(lane == 3, w2, 0.0))))
    route_ref[...] = route


def _mid(x2, gc2, gates2, o2, w_ao_bf, w_out_bf, g2, wr2, wrh, br, n_groups, epg, tm):
    n, d = x2.shape
    wo = o2.shape[1]
    row = lambda i: (i, 0)
    return pl.pallas_call(
        functools.partial(_mid_kernel, n_groups, epg), grid=(n // tm,),
        in_specs=[pl.BlockSpec((tm, d), row), pl.BlockSpec((tm, d), row),
                  pl.BlockSpec((tm, d), lambda i: (i, 1)), pl.BlockSpec((tm, wo), row),
                  _full(w_ao_bf.shape, True), _full(w_out_bf.shape, True), _full((1, d)),
                  _full(wr2.shape, True), _full(wrh.shape, True), _full(br.shape)],
        out_specs=[pl.BlockSpec((tm, d), row), pl.BlockSpec((tm, LANES), row)],
        out_shape=[jax.ShapeDtypeStruct((n, d), F32), jax.ShapeDtypeStruct((n, LANES), F32)],
        compiler_params=_cparams(("parallel",)),
        name="mid")(x2, gc2, gates2, o2, w_ao_bf, w_out_bf, g2, wr2, wrh, br)


def _dispatch_kernel(slot_ref, x1_ref, g2_ref, xs_in_hbm, xs_hbm, xn_ref, sem):
    del xs_in_hbm
    tmd = x1_ref.shape[0]
    xn_ref[...] = _rms(x1_ref[...], g2_ref[...])

    def row_copy(r, s):
        return pltpu.make_async_copy(xn_ref.at[pl.ds(r, 1)], xs_hbm.at[pl.ds(s, 1)], sem)

    for r in range(tmd):
        row_copy(r, slot_ref[0, 0, 2 * r]).start(priority=0)
        row_copy(r, slot_ref[0, 0, 2 * r + 1]).start(priority=1)

    for _ in range(2):
        pltpu.make_async_copy(xn_ref, xs_hbm.at[pl.ds(0, tmd)], sem).wait()


def _dispatch(slots, x1, g2, xs, tmd):
    n, d = x1.shape
    nt = n // tmd
    return pl.pallas_call(
        _dispatch_kernel, grid=(nt,),
        in_specs=[pl.BlockSpec((1, 1, 2 * tmd), lambda i: (i, 0, 0), memory_space=pltpu.SMEM),
                  pl.BlockSpec((tmd, d), lambda i: (i, 0)), _full((1, d)),
                  pl.BlockSpec(memory_space=pl.ANY)],
        out_specs=pl.BlockSpec(memory_space=pl.ANY),
        out_shape=jax.ShapeDtypeStruct(xs.shape, xs.dtype),
        scratch_shapes=[pltpu.VMEM((tmd, d), F32), pltpu.SemaphoreType.DMA],
        input_output_aliases={3: 0},
        compiler_params=_cparams(("arbitrary",)),
        name="dispatch")(slots.reshape(nt, 1, 2 * tmd), x1, g2, xs)


def _expert_kernel(te_ref, nt_ref, xs_ref, wg_ref, wu_ref, wd_ref, ys_ref, wgb_ref, wub_ref, wdb_ref):
    i = pl.program_id(0)
    used = i < nt_ref[0]
    new_expert = (i == 0) | (te_ref[i] != te_ref[jnp.maximum(i - 1, 0)])

    @pl.when(used & new_expert)
    def _():
        wgb_ref[...] = wg_ref[0].astype(BF16)
        wub_ref[...] = wu_ref[0].astype(BF16)
        wdb_ref[...] = wd_ref[0].astype(BF16)

    @pl.when(used)
    def _():
        x = xs_ref[...].astype(BF16)
        hg = jnp.dot(x, wgb_ref[...], preferred_element_type=F32)
        hu = jnp.dot(x, wub_ref[...], preferred_element_type=F32)
        h = (hg * jax.nn.sigmoid(hg)) * hu
        ys_ref[...] = jnp.dot(h.astype(BF16), wdb_ref[...], preferred_element_type=F32)

    @pl.when(jnp.logical_not(used))
    def _():
        ys_ref[...] = jnp.zeros_like(ys_ref)


def _experts(tile_expert, n_tiles, xs, wg, wu, wd, tme):
    s, d = xs.shape
    de = wg.shape[2]
    tile = lambda i, te, nt: (jnp.minimum(i, nt[0] - 1), 0)
    wmap = lambda i, te, nt: (te[jnp.minimum(i, nt[0] - 1)], 0, 0)
    grid_spec = pltpu.PrefetchScalarGridSpec(
        num_scalar_prefetch=2, grid=(s // tme,),
        in_specs=[pl.BlockSpec((tme, d), tile),
                  pl.BlockSpec((1, d, de), wmap), pl.BlockSpec((1, d, de), wmap), pl.BlockSpec((1, de, d), wmap)],
        out_specs=pl.BlockSpec((tme, d), lambda i, te, nt: (i, 0)),
        scratch_shapes=[pltpu.VMEM((d, de), BF16), pltpu.VMEM((d, de), BF16), pltpu.VMEM((de, d), BF16)])
    return pl.pallas_call(
        _expert_kernel, grid_spec=grid_spec,
        out_shape=jax.ShapeDtypeStruct((s, d), F32),
        compiler_params=_cparams(("arbitrary",)),
        name="experts")(tile_expert, n_tiles, xs, wg, wu, wd)


def _combine_kernel(slot_ref, x1_ref, route_ref, gf_ref, ys_hbm, y_ref, buf_ref, sem):
    tmf = x1_ref.shape[0]

    def row_copy(s, k, r):
        return pltpu.make_async_copy(ys_hbm.at[pl.ds(s, 1)], buf_ref.at[k, pl.ds(r, 1)], sem)

    for r in range(tmf):
        row_copy(slot_ref[0, 0, 2 * r], 0, r).start(priority=0)
        row_copy(slot_ref[0, 0, 2 * r + 1], 1, r).start(priority=1)

    for k in range(2):
        pltpu.make_async_copy(ys_hbm.at[pl.ds(0, tmf)], buf_ref.at[k], sem).wait()
    w1 = route_ref[:, 2:3]
    w2 = route_ref[:, 3:4]
    x2 = x1_ref[...] + (w1 * buf_ref[0] + w2 * buf_ref[1])
    y_ref[...] = _rms(x2, gf_ref[...])


def _combine(slots, x1, route, gf, ys, tmf):
    n, d = x1.shape
    nt = n // tmf
    row = lambda i: (i, 0)
    return pl.pallas_call(
        _combine_kernel, grid=(nt,),
        in_specs=[pl.BlockSpec((1, 1, 2 * tmf), lambda i: (i, 0, 0), memory_space=pltpu.SMEM),
                  pl.BlockSpec((tmf, d), row), pl.BlockSpec((tmf, LANES), row), _full((1, d)),
                  pl.BlockSpec(memory_space=pl.ANY)],
        out_specs=pl.BlockSpec((tmf, d), row),
        out_shape=jax.ShapeDtypeStruct((n, d), F32),
        scratch_shapes=[pltpu.VMEM((2, tmf, d), F32), pltpu.SemaphoreType.DMA],
        compiler_params=_cparams(("arbitrary",)),
        name="combine")(slots.reshape(nt, 1, 2 * tmf), x1, route, gf, ys)


def _pick(n, pref):
    t = min(n, pref)
    while n % t:
        t //= 2
    return t


def _mixer(x3, hist, k_past_t, v_past, pos, p, n_heads, lam_init):
    b, t, d = x3.shape
    n = b * t
    x2 = x3.reshape(n, d)
    c_conv = p["w_dw"].shape[1]
    prompt = k_past_t is None
    tm = _pick(n, 512)
    tabs = _rope_tables(pos, max(t, tm))
    q_scale = (d // n_heads // 4) ** -0.5 * LOG2E
    outs = _in_proj(x2, p["norm1_g"], p["w_cat"], p["b_cat"], tabs, q_scale, tm, t, prompt)
    if prompt:
        glu, q, k_bf, k_t, v, v_t, gates = outs
    else:
        glu, q, k, v, gates = outs

    glu3 = glu.reshape(b, t, c_conv)
    gates3 = gates.reshape(b, t, 2 * d)
    gc = _conv_branch(glu3, hist, p["w_dw"], p["b_dw"], p["ln_g"], p["ln_b"], p["w_co"], p["b_co"],
                      gates3, _pick(t, 512))

    w = q.shape[1]
    q3 = q.reshape(b, t, w)
    hd = w // n_heads // 2
    if prompt:
        o = _attn_prompt(q3, k_bf.reshape(b, t, w), v_t, p["lam4"], p["subln_g"].reshape(-1, 1), lam_init,
                         2 * KV_BLOCK)
        k_out = k_t.reshape(b, n_heads, 2, hd, t).transpose(0, 4, 1, 2, 3)[None]
    else:
        k3 = k.reshape(b, t, w)
        o = _attn_sample(q3, k_past_t, v_past, k3, v.reshape(b, t, w), p["lam4"], p["subln_g"], lam_init, n_heads)
        k_out = k3.reshape(1, b, t, n_heads, 2, hd)
    v_out = v.reshape(1, b, t, n_heads, w // n_heads)

    x1, route = _mid(x2, gc.reshape(n, d), gates, o.reshape(n, w), p["w_ao"], p["w_out"],
                     p["norm2_g"], p["wr2"], p["wrh"], p["br"], p["n_groups"], p["epg"], _pick(n, 512))
    new_hist = glu3[:, t - (p["w_dw"].shape[0] - 1):, :]
    return x1, route, k_out, v_out, new_hist[None]


def kernel(x_prompt, x_sample, cache_k, cache_v, state_conv, norm1_g, w_in, b_glu, w_dw, b_dw, conv_ln_g, conv_ln_b, w_conv_out, b_conv_out, lambda_q1, lambda_k1, lambda_q2, lambda_k2, subln_g, w_attn_out, w_out, norm2_g, w_router_group, b_router_group, w_router_expert, b_router_expert, w_exp_gate, w_exp_up, w_exp_down, final_norm_g):
    bp, tp, d = x_prompt.shape
    bs, ts, _ = x_sample.shape
    assert w_in.shape[0] == 1
    l = 0
    past = cache_k.shape[2]
    n_heads = cache_k.shape[3]
    hd = cache_k.shape[5]
    v_dim = cache_v.shape[4]
    conv_w = w_dw.shape[1]
    c_conv = w_dw.shape[2]
    n_groups = w_router_group.shape[2]
    n_exp = w_router_expert.shape[2]
    epg = n_exp // n_groups
    lam_init = 0.8 - 0.6 * math.exp(-0.3 * l)
    assert c_conv == 1024 and n_heads * 2 * hd == 1024 and n_heads * v_dim == 1024 and tp % (2 * KV_BLOCK) == 0

    hc = c_conv // 2
    wl = w_in[l]
    w_cat = jnp.concatenate([wl[:, :hc], wl[:, c_conv:c_conv + hc], wl[:, hc:c_conv], wl[:, c_conv + hc:]],
                            axis=1).astype(BF16)
    bg = b_glu[l]
    b_cat = jnp.concatenate([bg[:hc], bg[c_conv:c_conv + hc], bg[hc:c_conv], bg[c_conv + hc:]])[None]

    wr = jnp.zeros((d, LANES), F32)
    wr = wr.at[:, :n_exp].set(w_router_expert[l]).at[:, n_exp:n_exp + n_groups].set(w_router_group[l])
    wrh = wr.astype(BF16)
    wrl = (wr - wrh.astype(F32)).astype(BF16)
    br = jnp.zeros((1, LANES), F32)
    br = br.at[0, :n_exp].set(b_router_expert[l]).at[0, n_exp:n_exp + n_groups].set(b_router_group[l])
    p = dict(
        norm1_g=norm1_g[l][None], w_cat=w_cat, b_cat=b_cat,
        w_dw=w_dw[l], b_dw=b_dw[l][None], ln_g=conv_ln_g[l][None], ln_b=conv_ln_b[l][None],
        w_co=w_conv_out[l].astype(BF16), b_co=b_conv_out[l][None],
        lam4=jnp.stack([lambda_q1[l], lambda_k1[l], lambda_q2[l], lambda_k2[l]]),
        subln_g=subln_g[l][None], w_ao=w_attn_out[l].astype(BF16), w_out=w_out[l].astype(BF16),
        norm2_g=norm2_g[l][None], wr2=jnp.concatenate([wrh, wrl], axis=1), wrh=wrh, br=br,
        n_groups=n_groups, epg=epg)

    hist_p = jnp.zeros((bp, HIST_PAD, c_conv), F32)
    hist_s = jnp.concatenate(
        [jnp.zeros((bs, HIST_PAD - (conv_w - 1), c_conv), F32), state_conv[l]], axis=1)
    ckt = cache_k[l].transpose(0, 2, 3, 4, 1).reshape(bs, n_heads * 2 * hd, past)
    cv = cache_v[l].reshape(bs, past * n_heads, v_dim)

    x1p, routep, kp, vp, cp = _mixer(x_prompt, hist_p, None, None, jnp.arange(tp), p, n_heads, lam_init)
    x1s, routes, ks, vs, cs = _mixer(x_sample, hist_s, ckt, cv, past + jnp.arange(ts), p, n_heads, lam_init)

    np_, ns_ = bp * tp, bs * ts
    n = np_ + ns_
    tme = 512
    e_idx = jnp.concatenate([routep[:, :2], routes[:, :2]], axis=0).astype(I32)
    onehot = (e_idx[:, :, None] == jnp.arange(n_exp, dtype=I32)[None, None, :]).astype(I32)
    per_tok = onehot.sum(axis=1)
    before = jnp.cumsum(per_tok, axis=0) - per_tok
    rank = jnp.sum(onehot * before[:, None, :], axis=-1)
    counts = per_tok.sum(axis=0)
    tiles_per = (counts + tme - 1) // tme
    tile_end = jnp.cumsum(tiles_per)
    tile_start = tile_end - tiles_per
    slots = (jnp.sum(onehot * (tile_start * tme)[None, None, :], axis=-1) + rank).astype(I32)
    max_tiles = (2 * n) // tme + n_exp
    tile_ids = jnp.arange(max_tiles, dtype=I32)
    tile_expert = jnp.minimum(jnp.sum(tile_ids[:, None] >= tile_end[None, :], axis=1), n_exp - 1).astype(I32)
    n_tiles = tile_end[-1:].astype(I32)

    g2 = norm2_g[l][None]
    xs = jnp.zeros((max_tiles * tme, d), F32)
    xs = _dispatch(slots[:np_].reshape(-1), x1p, g2, xs, _pick(np_, 256))
    xs = _dispatch(slots[np_:].reshape(-1), x1s, g2, xs, _pick(ns_, 256))
    ys = _experts(tile_expert, n_tiles, xs, w_exp_gate[l], w_exp_up[l], w_exp_down[l], tme)
    gf = final_norm_g[None]
    yp = _combine(slots[:np_].reshape(-1), x1p, routep, gf, ys, _pick(np_, 256))
    ysm = _combine(slots[np_:].reshape(-1), x1s, routes, gf, ys, _pick(ns_, 256))

    return (yp.reshape(bp, tp, d), ysm.reshape(bs, ts, d), kp, vp, cp, ks, vs, cs)
```

```python
import functools
import math

import jax
import jax.numpy as jnp
from jax import lax
from jax.experimental import pallas as pl
from jax.experimental.pallas import tpu as pltpu

F32 = jnp.float32
BF16 = jnp.bfloat16
I32 = jnp.int32

EPS = 1e-6
CHUNK = 64
ROPE_DIM = 16
ROPE_THETA = 500000.0
LOG2E = 1.4426950408889634
LANES = 128
SUBLANES = 8
KV_BLOCK = 256
HIST_PAD = 32
VMEM_LIMIT = 56 * 1024 * 1024


def _cparams(sem):
    return pltpu.CompilerParams(dimension_semantics=sem, vmem_limit_bytes=VMEM_LIMIT)


def _full(shape, single=False):
    idx = lambda *_: (0,) * len(shape)
    if single:
        return pl.BlockSpec(shape, idx, pipeline_mode=pl.Buffered(1))
    return pl.BlockSpec(shape, idx)


def _rms(x, g):
    return x * lax.rsqrt(jnp.mean(x * x, axis=-1, keepdims=True) + EPS) * g


def _rms_rows(x_ref, g_ref, xn_ref):
    tm = x_ref.shape[0]
    rc = min(tm, 128)

    def body(r, carry):
        r0 = pl.multiple_of(r * rc, rc)
        xn_ref[pl.ds(r0, rc), :] = _rms(x_ref[pl.ds(r0, rc), :], g_ref[...]).astype(xn_ref.dtype)
        return carry

    lax.fori_loop(0, tm // rc, body, 0)


def _rope(z, c_ref, s1_ref, s2_ref):
    outs = []
    for c in range(z.shape[1] // LANES):
        zc = z[:, c * LANES:(c + 1) * LANES]
        up = pltpu.roll(zc, LANES - ROPE_DIM // 2, axis=1)
        dn = pltpu.roll(zc, ROPE_DIM // 2, axis=1)
        outs.append(zc * c_ref[...] + up * s1_ref[...] + dn * s2_ref[...])
    return jnp.concatenate(outs, axis=1)


def _in_kernel(prompt, q_scale, x_ref, g_ref, w_ref, b_ref, c_ref, s1_ref, s2_ref, *rest):
    if prompt:
        glu_ref, q_ref, k_ref, kt_ref, v_ref, vt_ref, gate_ref, xn_ref = rest
    else:
        glu_ref, q_ref, k_ref, v_ref, gate_ref, xn_ref = rest
    j = pl.program_id(1)
    xn_ref = xn_ref.at[pl.program_id(2)]
    tm, tn = x_ref.shape[0], w_ref.shape[1]
    cw = 2 * LANES
    nchunk = tn // cw

    @pl.when(j == 0)
    def _():
        _rms_rows(x_ref, g_ref, xn_ref)

    def zchunk(c0):
        return jnp.dot(xn_ref[...], w_ref[:, c0:c0 + cw], preferred_element_type=F32)

    @pl.when(j < 2)
    def _():
        half = tn // 2
        for c in range(half // cw):
            a = zchunk(c * cw) + b_ref[:, c * cw:(c + 1) * cw]
            b = zchunk(half + c * cw) + b_ref[:, half + c * cw:half + (c + 1) * cw]
            glu_ref[:, c * cw:(c + 1) * cw] = a * jax.nn.sigmoid(b)

    @pl.when(j == 2)
    def _():
        for c in range(nchunk):
            zr = _rope(zchunk(c * cw), c_ref, s1_ref, s2_ref)
            q_ref[:, c * cw:(c + 1) * cw] = (zr * q_scale).astype(q_ref.dtype)

    @pl.when(j == 3)
    def _():
        for c in range(nchunk):
            zr = _rope(zchunk(c * cw), c_ref, s1_ref, s2_ref)
            k_ref[:, c * cw:(c + 1) * cw] = zr.astype(k_ref.dtype)
            if prompt:
                kt_ref[0, c * cw:(c + 1) * cw, :] = zr.T

    @pl.when(j == 4)
    def _():
        for c in range(nchunk):
            z = zchunk(c * cw)
            v_ref[:, c * cw:(c + 1) * cw] = z
            if prompt:
                kb = vt_ref.shape[4]
                for hh in range(cw // LANES):
                    for blk in range(tm // kb):
                        vt_ref[0, c * (cw // LANES) + hh, blk] = (
                            z[blk * kb:(blk + 1) * kb, hh * LANES:(hh + 1) * LANES].T.astype(BF16))

    @pl.when(j >= 5)
    def _():
        for c in range(nchunk):
            gate_ref[:, c * cw:(c + 1) * cw] = jax.nn.sigmoid(zchunk(c * cw)).astype(gate_ref.dtype)


def _in_proj(x2, g, w_cat, b_cat, tabs, q_scale, tm, seq, prompt):
    n, d = x2.shape
    tn = 1024
    nj = w_cat.shape[1] // tn
    ngate = nj - 5
    nper = tabs[0].shape[0] // tm
    grp = 2 if (n // tm) % 2 == 0 else 1

    def rows_at(j_lo, j_hi):
        def row(io, j, ii):
            return io * grp + jnp.where(j < j_lo, 0, jnp.where(j <= j_hi, ii, grp - 1))
        return row

    x_row = rows_at(0, 0)
    glu_row, rope_row = rows_at(0, 1), rows_at(2, 3)
    q_row, k_row, v_row, gate_row = rows_at(2, 2), rows_at(3, 3), rows_at(4, 4), rows_at(5, nj - 1)
    in_specs = [pl.BlockSpec((tm, d), lambda io, j, ii: (x_row(io, j, ii), 0)), _full((1, d)),
                pl.BlockSpec((d, tn), lambda io, j, ii: (0, j)),
                pl.BlockSpec((1, tn), lambda io, j, ii: (0, jnp.minimum(j, 1)))]
    in_specs += [pl.BlockSpec((tm, LANES), lambda io, j, ii: (rope_row(io, j, ii) % nper, 0))] * 3
    glu_spec = pl.BlockSpec((tm, tn // 2), lambda io, j, ii: (glu_row(io, j, ii), jnp.minimum(j, 1)))
    gate_spec = pl.BlockSpec((tm, tn), lambda io, j, ii: (gate_row(io, j, ii), jnp.clip(j - 5, 0, ngate - 1)))

    def row_spec(row):
        return pl.BlockSpec((tm, tn), lambda io, j, ii: (row(io, j, ii), 0))

    sds = jax.ShapeDtypeStruct
    if prompt:
        tpb = seq // tm
        nb = n // seq

        def t_spec(shape, row):
            def idx(io, j, ii):
                r = row(io, j, ii)
                return (r // tpb, 0, r % tpb) + (0,) * (len(shape) - 3)
            return pl.BlockSpec(shape, idx)

        out_specs = [glu_spec, row_spec(q_row), row_spec(k_row), t_spec((1, tn, tm), k_row), row_spec(v_row),
                     t_spec((1, tn // LANES, tm // KV_BLOCK, LANES, KV_BLOCK), v_row), gate_spec]
        out_shape = [sds((n, tn), F32), sds((n, tn), BF16), sds((n, tn), BF16), sds((nb, tn, seq), F32),
                     sds((n, tn), F32), sds((nb, tn // LANES, seq // KV_BLOCK, LANES, KV_BLOCK), BF16),
                     sds((n, ngate * tn), BF16)]
    else:
        out_specs = [glu_spec, row_spec(q_row), row_spec(k_row), row_spec(v_row), gate_spec]
        out_shape = [sds((n, tn), F32), sds((n, tn), BF16), sds((n, tn), F32), sds((n, tn), F32),
                     sds((n, ngate * tn), BF16)]
    return pl.pallas_call(
        functools.partial(_in_kernel, prompt, q_scale), grid=(n // (tm * grp), nj, grp),
        in_specs=in_specs, out_specs=out_specs, out_shape=out_shape,
        scratch_shapes=[pltpu.VMEM((grp, tm, d), BF16)],
        compiler_params=_cparams(("parallel", "arbitrary", "arbitrary")),
        name="in_proj")(x2, g, w_cat, b_cat, *tabs)


def _rope_tables(pos, rows):
    half = ROPE_DIM // 2
    inv_freq = 1.0 / (ROPE_THETA ** (jnp.arange(0, ROPE_DIM, 2, dtype=F32) / ROPE_DIM))
    ang = pos.astype(F32)[:, None] * inv_freq[None, :]
    cos, sin = jnp.cos(ang), jnp.sin(ang)
    t = pos.shape[0]
    sub = 64
    ones = jnp.ones((t, sub - ROPE_DIM), F32)
    zeros = jnp.zeros((t, sub - ROPE_DIM), F32)
    zh = jnp.zeros((t, half), F32)
    c = jnp.concatenate([cos, cos, ones], axis=1)
    s1 = jnp.concatenate([-sin, zh, zeros], axis=1)
    s2 = jnp.concatenate([zh, sin, zeros], axis=1)
    reps = (rows // t, LANES // sub)
    return tuple(jnp.tile(a, reps) for a in (c, s1, s2))


def _conv_kernel(conv_w, glu_ref, hist_ref, wdw_ref, bdw_ref, lng_ref, lnb_ref, wco_ref, bco_ref,
                 gate_ref, o_ref, xs_ref, acc_ref, c_ref):
    tt = glu_ref.shape[1]
    rows = HIST_PAD + tt
    i = pl.program_id(1)

    @pl.when(i == 0)
    def _():
        xs_ref[0, 0:HIST_PAD, :] = hist_ref[0]

    @pl.when(i > 0)
    def _():
        xs_ref[0, 0:HIST_PAD, :] = xs_ref[0, tt:tt + HIST_PAD, :]

    xs_ref[0, HIST_PAD:rows, :] = glu_ref[0]
    for b in range(1, SUBLANES):
        xs_ref[b, 0:rows - SUBLANES, :] = xs_ref[0, b:b + rows - SUBLANES, :]

    rc = 2 * SUBLANES
    off = HIST_PAD - (conv_w - 1)

    def chunk(r, carry):
        r0 = pl.multiple_of(r * rc, rc)
        accs = [None] * (rc // SUBLANES)
        for j in range(conv_w):
            a, b = divmod(off + j, SUBLANES)
            wj = wdw_ref[j]
            for u in range(len(accs)):
                start = pl.multiple_of(r0 + (a + u) * SUBLANES, SUBLANES)
                term = xs_ref[b, pl.ds(start, SUBLANES), :] * wj
                accs[u] = term if accs[u] is None else accs[u] + term
        acc_ref[pl.ds(r0, rc), :] = jnp.concatenate(accs, axis=0)
        return carry

    lax.fori_loop(0, tt // rc, chunk, 0)
    acc = acc_ref[...] + bdw_ref[...]
    mu = jnp.mean(acc, axis=-1, keepdims=True)
    xc = acc - mu
    var = jnp.mean(xc * xc, axis=-1, keepdims=True)
    y = xc * lax.rsqrt(var + EPS) * lng_ref[...] + lnb_ref[...]
    c_ref[...] = (y * jax.nn.sigmoid(y)).astype(BF16)
    out = jnp.dot(c_ref[...], wco_ref[...], preferred_element_type=F32) + bco_ref[...]
    o_ref[0] = (gate_ref[0].astype(F32) * out).astype(o_ref.dtype)


def _conv_branch(glu3, hist, w_dw, b_dw, ln_g, ln_b, w_co_bf, b_co, gates3, tt):
    b, t, c = glu3.shape
    d = w_co_bf.shape[1]
    conv_w = w_dw.shape[0]
    return pl.pallas_call(
        functools.partial(_conv_kernel, conv_w), grid=(b, t // tt),
        in_specs=[pl.BlockSpec((1, tt, c), lambda bi, i: (bi, i, 0)),
                  pl.BlockSpec((1, HIST_PAD, c), lambda bi, i: (bi, 0, 0)),
                  _full((conv_w, SUBLANES, c)), _full((1, c)), _full((1, c)), _full((1, c)),
                  _full((c, d)), _full((1, d)),
                  pl.BlockSpec((1, tt, d), lambda bi, i: (bi, i, 0))],
        out_specs=pl.BlockSpec((1, tt, d), lambda bi, i: (bi, i, 0)),
        out_shape=jax.ShapeDtypeStruct((b, t, d), BF16),
        scratch_shapes=[pltpu.VMEM((SUBLANES, HIST_PAD + tt, c), F32), pltpu.VMEM((tt, c), F32),
                        pltpu.VMEM((tt, c), BF16)],
        compiler_params=_cparams(("parallel", "arbitrary")),
        name="conv_branch")(glu3, hist, jnp.broadcast_to(w_dw[:, None, :], (conv_w, SUBLANES, c)), b_dw, ln_g, ln_b,
                            w_co_bf, b_co, gates3)


def _lambda(lam_ref, lam_init):
    lq1, lk1, lq2, lk2 = (lam_ref[r:r + 1, :] for r in range(4))
    return (jnp.exp(jnp.sum(lq1 * lk1, axis=-1, keepdims=True))
            - jnp.exp(jnp.sum(lq2 * lk2, axis=-1, keepdims=True)) + lam_init)


def _split_maps(q):
    lane = lax.broadcasted_iota(I32, q.shape, 1)
    zero = jnp.zeros_like(q)
    half = q.shape[1] // 2
    return jnp.where(lane < half, q, zero), jnp.where(lane >= half, q, zero)


def _nt(a, b):
    return lax.dot_general(a, b, (((1,), (1,)), ((), ())), preferred_element_type=F32)


def _attn_prompt_kernel(lam_init, q_ref, k_ref, vt_ref, lam_ref, sgt_ref, o_ref, acc_ref, sa_ref, sb_ref):
    tq = q_ref.shape[1]
    tk = vt_ref.shape[4]
    assert tq == 2 * tk
    qi = pl.program_id(2)
    qz = _split_maps(q_ref[0])
    acc_ref[...] = jnp.zeros_like(acc_ref)
    cdiff = (lax.broadcasted_iota(I32, (tk, tq), 0) // CHUNK) - (lax.broadcasted_iota(I32, (tk, tq), 1) // CHUNK)

    def scores(kb, s_ref):
        k0 = pl.multiple_of(kb * tk, tk)
        kblk = k_ref[0, pl.ds(k0, tk), :]
        for m in range(2):
            s_ref[m] = _nt(kblk, qz[m])

    def consume(kb, s_ref, carry, diag_block):
        vblk = vt_ref[0, 0, kb]
        new, alphas, ps = [], [], []
        for m in range(2):
            mx, l = carry[m]
            if diag_block is None:
                sm = s_ref[m]
            else:
                sm = jnp.where(cdiff <= -diag_block * (tk // CHUNK), s_ref[m], -jnp.inf)
            mn = jnp.maximum(mx, jnp.max(sm, axis=0, keepdims=True))
            alpha = jnp.exp2(mx - mn)
            p = jnp.exp2(sm - mn)
            new.append((mn, alpha * l + jnp.sum(p, axis=0, keepdims=True)))
            alphas.append(alpha)
            ps.append(p.astype(BF16))
        pv = [jnp.dot(vblk, ps[m], preferred_element_type=F32) for m in range(2)]
        for m in range(2):
            acc_ref[m] = alphas[m] * acc_ref[m] + pv[m]
        return tuple(new)

    def pair(j, carry):
        scores(2 * j + 1, sb_ref)
        carry = consume(2 * j, sa_ref, carry, None)
        scores(2 * j + 2, sa_ref)
        return consume(2 * j + 1, sb_ref, carry, None)

    init = tuple((jnp.full((1, tq), -jnp.inf, F32), jnp.zeros((1, tq), F32)) for _ in range(2))
    scores(0, sa_ref)
    carry = lax.fori_loop(0, qi, pair, init)
    scores(2 * qi + 1, sb_ref)
    carry = consume(2 * qi, sa_ref, carry, 0)
    carry = consume(2 * qi + 1, sb_ref, carry, 1)
    lam = _lambda(lam_ref, lam_init)
    ot = acc_ref[0] * (1.0 / carry[0][1]) - lam * (acc_ref[1] * (1.0 / carry[1][1]))
    ms = jnp.mean(ot * ot, axis=0, keepdims=True)
    yt = ot * lax.rsqrt(ms + EPS) * sgt_ref[...] * (1.0 - lam_init)
    o_ref[0] = yt.T.astype(o_ref.dtype)


def _attn_prompt(q3, k3, vt5, lam4, sgt, lam_init, tq):
    b, t, w = q3.shape
    n_heads = vt5.shape[1]
    hw = w // n_heads
    blk = lambda bi, h, i: (bi, i, h)
    return pl.pallas_call(
        functools.partial(_attn_prompt_kernel, lam_init), grid=(b, n_heads, t // tq),
        in_specs=[pl.BlockSpec((1, tq, hw), blk),
                  pl.BlockSpec((1, t, hw), lambda bi, h, i: (bi, 0, h)),
                  pl.BlockSpec((1, 1) + vt5.shape[2:], lambda bi, h, i: (bi, h, 0, 0, 0)),
                  _full(lam4.shape), _full(sgt.shape)],
        out_specs=pl.BlockSpec((1, tq, hw), blk),
        out_shape=jax.ShapeDtypeStruct((b, t, w), BF16),
        scratch_shapes=[pltpu.VMEM((2, vt5.shape[3], tq), F32), pltpu.VMEM((2, KV_BLOCK, tq), F32),
                        pltpu.VMEM((2, KV_BLOCK, tq), F32)],
        compiler_params=_cparams(("parallel", "parallel", "arbitrary")),
        name="attn_prompt")(q3, k3, vt5, lam4, sgt)


def _attn_sample_kernel(lam_init, n_heads, q_ref, ckt_ref, cv_ref, kn_ref, vn_ref, lam_ref, sg_ref, o_ref):
    past = ckt_ref.shape[2]
    hw = q_ref.shape[2] // n_heads
    lam = _lambda(lam_ref, lam_init)
    for h in range(n_heads):
        cols = slice(h * hw, (h + 1) * hw)
        qs = jnp.concatenate(_split_maps(q_ref[0, :, cols]), axis=0)
        t = qs.shape[0] // 2
        ckt = ckt_ref[0, cols, :].astype(BF16)
        cv = cv_ref[0, pl.ds(h, past, stride=n_heads), :].astype(BF16)
        kn = kn_ref[0, :, cols].astype(BF16)
        vn = vn_ref[0, :, cols].astype(BF16)
        sp = jnp.dot(qs, ckt, preferred_element_type=F32)
        sn = _nt(qs, kn)
        mx = jnp.maximum(jnp.max(sp, axis=-1, keepdims=True), jnp.max(sn, axis=-1, keepdims=True))
        pp = jnp.exp2(sp - mx)
        pn = jnp.exp2(sn - mx)
        l = jnp.sum(pp, axis=-1, keepdims=True) + jnp.sum(pn, axis=-1, keepdims=True)
        acc = (jnp.dot(pp.astype(BF16), cv, preferred_element_type=F32)
               + jnp.dot(pn.astype(BF16), vn, preferred_element_type=F32))
        on = acc / l
        o = on[:t] - lam * on[t:]
        y = _rms(o, sg_ref[...]) * (1.0 - lam_init)
        o_ref[0, :, cols] = y.astype(o_ref.dtype)


def _attn_sample(q3, ckt3, cv3, kn3, vn3, lam4, subln_g, lam_init, n_heads):
    b, t, w = q3.shape
    idx = lambda bi: (bi, 0, 0)
    return pl.pallas_call(
        functools.partial(_attn_sample_kernel, lam_init, n_heads), grid=(b,),
        in_specs=[pl.BlockSpec((1, t, w), idx), pl.BlockSpec((1,) + ckt3.shape[1:], idx),
                  pl.BlockSpec((1,) + cv3.shape[1:], idx), pl.BlockSpec((1, t, w), idx),
                  pl.BlockSpec((1, t, w), idx), _full(lam4.shape), _full(subln_g.shape)],
        out_specs=pl.BlockSpec((1, t, w), idx),
        out_shape=jax.ShapeDtypeStruct((b, t, w), BF16),
        compiler_params=_cparams(("parallel",)),
        name="attn_sample")(q3, ckt3, cv3, kn3, vn3, lam4, subln_g)


def _mid_kernel(n_groups, epg, x_ref, gc_ref, ga_ref, o_ref, wao_ref, wout_ref, g2_ref, wr2_ref, wrh_ref,
                br_ref, x1_ref, route_ref):
    attn = jnp.dot(o_ref[...], wao_ref[...], preferred_element_type=F32)
    merged = gc_ref[...].astype(F32) + ga_ref[...].astype(F32) * attn
    x1 = x_ref[...] + jnp.dot(merged.astype(BF16), wout_ref[...], preferred_element_type=F32)
    x1_ref[...] = x1
    xn2 = _rms(x1, g2_ref[...])

    n_exp = n_groups * epg
    hi = xn2.astype(BF16)
    lo = (xn2 - hi.astype(F32)).astype(BF16)
    both = jnp.dot(hi, wr2_ref[...], preferred_element_type=F32)
    lg = (both[:, :LANES] + both[:, LANES:]
          + jnp.dot(lo, wrh_ref[...], preferred_element_type=F32) + br_ref[...])
    lane = lax.broadcasted_iota(I32, lg.shape, 1)
    neg = -jnp.inf
    big = jnp.int32(2 ** 30)
    is_g = (lane >= n_exp) & (lane < n_exp + n_groups)
    gl = jnp.where(is_g, lg, neg)
    gmax = jnp.max(gl, axis=-1, keepdims=True)
    g_idx = jnp.min(jnp.where(gl == gmax, lane, big), axis=-1, keepdims=True) - n_exp
    p_g = 1.0 / jnp.sum(jnp.where(is_g, jnp.exp(gl - gmax), 0.0), axis=-1, keepdims=True)
    in_grp = (lane >= g_idx * epg) & (lane < (g_idx + 1) * epg)
    el = jnp.where(in_grp, lg, neg)
    v1 = jnp.max(el, axis=-1, keepdims=True)
    e1 = jnp.min(jnp.where(el == v1, lane, big), axis=-1, keepdims=True)
    el2 = jnp.where(lane == e1, neg, el)
    v2 = jnp.max(el2, axis=-1, keepdims=True)
    e2 = jnp.min(jnp.where(el2 == v2, lane, big), axis=-1, keepdims=True)
    t2 = jnp.exp(v2 - v1)
    den = 1.0 + t2
    w1 = p_g / den
    w2 = p_g * t2 / den
    route = jnp.where(lane == 0, e1.astype(F32),
                      jnp.where(lane == 1, e2.astype(F32),
                                jnp.where(lane == 2, w1, jnp.where(lane == 3, w2, 0.0))))
    route_ref[...] = route


def _mid(x2, gc2, gates2, o2, w_ao_bf, w_out_bf, g2, wr2, wrh, br, n_groups, epg, tm):
    n, d = x2.shape
    wo = o2.shape[1]
    row = lambda i: (i, 0)
    return pl.pallas_call(
        functools.partial(_mid_kernel, n_groups, epg), grid=(n // tm,),
        in_specs=[pl.BlockSpec((tm, d), row), pl.BlockSpec((tm, d), row),
                  pl.BlockSpec((tm, d), lambda i: (i, 1)), pl.BlockSpec((tm, wo), row),
                  _full(w_ao_bf.shape, True), _full(w_out_bf.shape, True), _full((1, d)),
                  _full(wr2.shape, True), _full(wrh.shape, True), _full(br.shape)],
        out_specs=[pl.BlockSpec((tm, d), row), pl.BlockSpec((tm, LANES), row)],
        out_shape=[jax.ShapeDtypeStruct((n, d), F32), jax.ShapeDtypeStruct((n, LANES), F32)],
        compiler_params=_cparams(("parallel",)),
        name="mid")(x2, gc2, gates2, o2, w_ao_bf, w_out_bf, g2, wr2, wrh, br)


def _dispatch_kernel(slot_ref, x1_ref, g2_ref, xs_in_hbm, xs_hbm, xn_ref, sem):
    del xs_in_hbm
    tmd = x1_ref.shape[0]
    xn_ref[...] = _rms(x1_ref[...], g2_ref[...])

    def row_copy(r, s):
        return pltpu.make_async_copy(xn_ref.at[pl.ds(r, 1)], xs_hbm.at[pl.ds(s, 1)], sem)

    for r in range(tmd):
        row_copy(r, slot_ref[0, 0, 2 * r]).start(priority=0)
        row_copy(r, slot_ref[0, 0, 2 * r + 1]).start(priority=1)

    for _ in range(2):
        pltpu.make_async_copy(xn_ref, xs_hbm.at[pl.ds(0, tmd)], sem).wait()


def _dispatch(slots, x1, g2, xs, tmd):
    n, d = x1.shape
    nt = n // tmd
    return pl.pallas_call(
        _dispatch_kernel, grid=(nt,),
        in_specs=[pl.BlockSpec((1, 1, 2 * tmd), lambda i: (i, 0, 0), memory_space=pltpu.SMEM),
                  pl.BlockSpec((tmd, d), lambda i: (i, 0)), _full((1, d)),
                  pl.BlockSpec(memory_space=pl.ANY)],
        out_specs=pl.BlockSpec(memory_space=pl.ANY),
        out_shape=jax.ShapeDtypeStruct(xs.shape, xs.dtype),
        scratch_shapes=[pltpu.VMEM((tmd, d), F32), pltpu.SemaphoreType.DMA],
        input_output_aliases={3: 0},
        compiler_params=_cparams(("arbitrary",)),
        name="dispatch")(slots.reshape(nt, 1, 2 * tmd), x1, g2, xs)


def _expert_kernel(te_ref, nt_ref, xs_ref, wg_ref, wu_ref, wd_ref, ys_ref, wgb_ref, wub_ref, wdb_ref):
    i = pl.program_id(0)
    used = i < nt_ref[0]
    new_expert = (i == 0) | (te_ref[i] != te_ref[jnp.maximum(i - 1, 0)])

    @pl.when(used & new_expert)
    def _():
        wgb_ref[...] = wg_ref[0].astype(BF16)
        wub_ref[...] = wu_ref[0].astype(BF16)
        wdb_ref[...] = wd_ref[0].astype(BF16)

    @pl.when(used)
    def _():
        x = xs_ref[...].astype(BF16)
        hg = jnp.dot(x, wgb_ref[...], preferred_element_type=F32)
        hu = jnp.dot(x, wub_ref[...], preferred_element_type=F32)
        h = (hg * jax.nn.sigmoid(hg)) * hu
        ys_ref[...] = jnp.dot(h.astype(BF16), wdb_ref[...], preferred_element_type=F32)

    @pl.when(jnp.logical_not(used))
    def _():
        ys_ref[...] = jnp.zeros_like(ys_ref)


def _experts(tile_expert, n_tiles, xs, wg, wu, wd, tme):
    s, d = xs.shape
    de = wg.shape[2]
    tile = lambda i, te, nt: (jnp.minimum(i, nt[0] - 1), 0)
    wmap = lambda i, te, nt: (te[jnp.minimum(i, nt[0] - 1)], 0, 0)
    grid_spec = pltpu.PrefetchScalarGridSpec(
        num_scalar_prefetch=2, grid=(s // tme,),
        in_specs=[pl.BlockSpec((tme, d), tile),
                  pl.BlockSpec((1, d, de), wmap), pl.BlockSpec((1, d, de), wmap), pl.BlockSpec((1, de, d), wmap)],
        out_specs=pl.BlockSpec((tme, d), lambda i, te, nt: (i, 0)),
        scratch_shapes=[pltpu.VMEM((d, de), BF16), pltpu.VMEM((d, de), BF16), pltpu.VMEM((de, d), BF16)])
    return pl.pallas_call(
        _expert_kernel, grid_spec=grid_spec,
        out_shape=jax.ShapeDtypeStruct((s, d), F32),
        compiler_params=_cparams(("arbitrary",)),
        name="experts")(tile_expert, n_tiles, xs, wg, wu, wd)


def _combine_kernel(nt, slot_ref, slot_next_ref, x1_ref, route_ref, gf_ref, ys_hbm, y_ref, buf_ref, sem):
    tmf = x1_ref.shape[0]
    i = pl.program_id(0)
    cur = i % 2

    def gather(s_ref, half):
        for r in range(tmf):
            for k in range(2):
                pltpu.make_async_copy(ys_hbm.at[pl.ds(s_ref[0, 0, 2 * r + k], 1)],
                                      buf_ref.at[half, k, pl.ds(r, 1)], sem.at[half]).start(priority=k)

    @pl.when(i == 0)
    def _():
        gather(slot_ref, 0)

    @pl.when(i + 1 < nt)
    def _():
        gather(slot_next_ref, 1 - cur)

    for k in range(2):
        pltpu.make_async_copy(ys_hbm.at[pl.ds(0, tmf)], buf_ref.at[cur, k], sem.at[cur]).wait()
    w1 = route_ref[:, 2:3]
    w2 = route_ref[:, 3:4]
    x2 = x1_ref[...] + (w1 * buf_ref[cur, 0] + w2 * buf_ref[cur, 1])
    y_ref[...] = _rms(x2, gf_ref[...])


def _combine(slots, x1, route, gf, ys, tmf):
    n, d = x1.shape
    nt = n // tmf
    row = lambda i: (i, 0)
    slots3 = slots.reshape(nt, 1, 2 * tmf)
    return pl.pallas_call(
        functools.partial(_combine_kernel, nt), grid=(nt,),
        in_specs=[pl.BlockSpec((1, 1, 2 * tmf), lambda i: (i, 0, 0), memory_space=pltpu.SMEM),
                  pl.BlockSpec((1, 1, 2 * tmf), lambda i: (jnp.minimum(i + 1, nt - 1), 0, 0),
                               memory_space=pltpu.SMEM),
                  pl.BlockSpec((tmf, d), row), pl.BlockSpec((tmf, LANES), row), _full((1, d)),
                  pl.BlockSpec(memory_space=pl.ANY)],
        out_specs=pl.BlockSpec((tmf, d), row),
        out_shape=jax.ShapeDtypeStruct((n, d), F32),
        scratch_shapes=[pltpu.VMEM((2, 2, tmf, d), F32), pltpu.SemaphoreType.DMA((2,))],
        compiler_params=_cparams(("arbitrary",)),
        name="combine")(slots3, slots3, x1, route, gf, ys)


def _pick(n, pref):
    t = min(n, pref)
    while n % t:
        t //= 2
    return t


def _mixer(x3, hist, k_past_t, v_past, pos, p, n_heads, lam_init):
    b, t, d = x3.shape
    n = b * t
    x2 = x3.reshape(n, d)
    c_conv = p["w_dw"].shape[1]
    prompt = k_past_t is None
    tm = _pick(n, 512)
    tabs = _rope_tables(pos, max(t, tm))
    q_scale = (d // n_heads // 4) ** -0.5 * LOG2E
    outs = _in_proj(x2, p["norm1_g"], p["w_cat"], p["b_cat"], tabs, q_scale, tm, t, prompt)
    if prompt:
        glu, q, k_bf, k_t, v, v_t, gates = outs
    else:
        glu, q, k, v, gates = outs

    glu3 = glu.reshape(b, t, c_conv)
    gates3 = gates.reshape(b, t, 2 * d)
    gc = _conv_branch(glu3, hist, p["w_dw"], p["b_dw"], p["ln_g"], p["ln_b"], p["w_co"], p["b_co"],
                      gates3, _pick(t, 512))

    w = q.shape[1]
    q3 = q.reshape(b, t, w)
    hd = w // n_heads // 2
    if prompt:
        o = _attn_prompt(q3, k_bf.reshape(b, t, w), v_t, p["lam4"], p["subln_g"].reshape(-1, 1), lam_init,
                         2 * KV_BLOCK)
        k_out = k_t.reshape(b, n_heads, 2, hd, t).transpose(0, 4, 1, 2, 3)[None]
    else:
        k3 = k.reshape(b, t, w)
        o = _attn_sample(q3, k_past_t, v_past, k3, v.reshape(b, t, w), p["lam4"], p["subln_g"], lam_init, n_heads)
        k_out = k3.reshape(1, b, t, n_heads, 2, hd)
    v_out = v.reshape(1, b, t, n_heads, w // n_heads)

    x1, route = _mid(x2, gc.reshape(n, d), gates, o.reshape(n, w), p["w_ao"], p["w_out"],
                     p["norm2_g"], p["wr2"], p["wrh"], p["br"], p["n_groups"], p["epg"], _pick(n, 512))
    new_hist = glu3[:, t - (p["w_dw"].shape[0] - 1):, :]
    return x1, route, k_out, v_out, new_hist[None]


def kernel(x_prompt, x_sample, cache_k, cache_v, state_conv, norm1_g, w_in, b_glu, w_dw, b_dw, conv_ln_g, conv_ln_b, w_conv_out, b_conv_out, lambda_q1, lambda_k1, lambda_q2, lambda_k2, subln_g, w_attn_out, w_out, norm2_g, w_router_group, b_router_group, w_router_expert, b_router_expert, w_exp_gate, w_exp_up, w_exp_down, final_norm_g):
    bp, tp, d = x_prompt.shape
    bs, ts, _ = x_sample.shape
    assert w_in.shape[0] == 1
    l = 0
    past = cache_k.shape[2]
    n_heads = cache_k.shape[3]
    hd = cache_k.shape[5]
    v_dim = cache_v.shape[4]
    conv_w = w_dw.shape[1]
    c_conv = w_dw.shape[2]
    n_groups = w_router_group.shape[2]
    n_exp = w_router_expert.shape[2]
    epg = n_exp // n_groups
    lam_init = 0.8 - 0.6 * math.exp(-0.3 * l)
    assert c_conv == 1024 and n_heads * 2 * hd == 1024 and n_heads * v_dim == 1024 and tp % (2 * KV_BLOCK) == 0

    hc = c_conv // 2
    wl = w_in[l]
    w_cat = jnp.concatenate([wl[:, :hc], wl[:, c_conv:c_conv + hc], wl[:, hc:c_conv], wl[:, c_conv + hc:]],
                            axis=1).astype(BF16)
    bg = b_glu[l]
    b_cat = jnp.concatenate([bg[:hc], bg[c_conv:c_conv + hc], bg[hc:c_conv], bg[c_conv + hc:]])[None]

    wr = jnp.zeros((d, LANES), F32)
    wr = wr.at[:, :n_exp].set(w_router_expert[l]).at[:, n_exp:n_exp + n_groups].set(w_router_group[l])
    wrh = wr.astype(BF16)
    wrl = (wr - wrh.astype(F32)).astype(BF16)
    br = jnp.zeros((1, LANES), F32)
    br = br.at[0, :n_exp].set(b_router_expert[l]).at[0, n_exp:n_exp + n_groups].set(b_router_group[l])
    p = dict(
        norm1_g=norm1_g[l][None], w_cat=w_cat, b_cat=b_cat,
        w_dw=w_dw[l], b_dw=b_dw[l][None], ln_g=conv_ln_g[l][None], ln_b=conv_ln_b[l][None],
        w_co=w_conv_out[l].astype(BF16), b_co=b_conv_out[l][None],
        lam4=jnp.stack([lambda_q1[l], lambda_k1[l], lambda_q2[l], lambda_k2[l]]),
        subln_g=subln_g[l][None], w_ao=w_attn_out[l].astype(BF16), w_out=w_out[l].astype(BF16),
        norm2_g=norm2_g[l][None], wr2=jnp.concatenate([wrh, wrl], axis=1), wrh=wrh, br=br,
        n_groups=n_groups, epg=epg)

    hist_p = jnp.zeros((bp, HIST_PAD, c_conv), F32)
    hist_s = jnp.concatenate(
        [jnp.zeros((bs, HIST_PAD - (conv_w - 1), c_conv), F32), state_conv[l]], axis=1)
    ckt = cache_k[l].transpose(0, 2, 3, 4, 1).reshape(bs, n_heads * 2 * hd, past)
    cv = cache_v[l].reshape(bs, past * n_heads, v_dim)

    x1p, routep, kp, vp, cp = _mixer(x_prompt, hist_p, None, None, jnp.arange(tp), p, n_heads, lam_init)
    x1s, routes, ks, vs, cs = _mixer(x_sample, hist_s, ckt, cv, past + jnp.arange(ts), p, n_heads, lam_init)

    np_, ns_ = bp * tp, bs * ts
    n = np_ + ns_
    tme = 256
    e_idx = jnp.concatenate([routep[:, :2], routes[:, :2]], axis=0).astype(I32)
    onehot = (e_idx[:, :, None] == jnp.arange(n_exp, dtype=I32)[None, None, :]).astype(I32)
    per_tok = onehot.sum(axis=1)
    before = jnp.cumsum(per_tok, axis=0) - per_tok
    rank = jnp.sum(onehot * before[:, None, :], axis=-1)
    counts = per_tok.sum(axis=0)
    tiles_per = (counts + tme - 1) // tme
    tile_end = jnp.cumsum(tiles_per)
    tile_start = tile_end - tiles_per
    slots = (jnp.sum(onehot * (tile_start * tme)[None, None, :], axis=-1) + rank).astype(I32)
    max_tiles = (2 * n) // tme + n_exp
    tile_ids = jnp.arange(max_tiles, dtype=I32)
    tile_expert = jnp.minimum(jnp.sum(tile_ids[:, None] >= tile_end[None, :], axis=1), n_exp - 1).astype(I32)
    n_tiles = tile_end[-1:].astype(I32)

    g2 = norm2_g[l][None]
    xs = jnp.zeros((max_tiles * tme, d), F32)
    xs = _dispatch(slots[:np_].reshape(-1), x1p, g2, xs, _pick(np_, 256))
    xs = _dispatch(slots[np_:].reshape(-1), x1s, g2, xs, _pick(ns_, 256))
    ys = _experts(tile_expert, n_tiles, xs, w_exp_gate[l], w_exp_up[l], w_exp_down[l], tme)
    gf = final_norm_g[None]
    yp = _combine(slots[:np_].reshape(-1), x1p, routep, gf, ys, _pick(np_, 256))
    ysm = _combine(slots[np_:].reshape(-1), x1s, routes, gf, ys, _pick(ns_, 256))

    return (yp.reshape(bp, tp, d), ysm.reshape(bs, ts, d), kp, vp, cp, ks, vs, cs)
```

```python
import functools
import math

import jax
import jax.numpy as jnp
from jax import lax
from jax.experimental import pallas as pl
from jax.experimental.pallas import tpu as pltpu

F32 = jnp.float32
BF16 = jnp.bfloat16
I32 = jnp.int32

EPS = 1e-6
CHUNK = 64
ROPE_DIM = 16
ROPE_THETA = 500000.0
LOG2E = 1.4426950408889634
LANES = 128
SUBLANES = 8
KV_BLOCK = 256
HIST_PAD = 32
VMEM_LIMIT = 56 * 1024 * 1024


def _cparams(sem):
    return pltpu.CompilerParams(dimension_semantics=sem, vmem_limit_bytes=VMEM_LIMIT)


def _full(shape, single=False):
    idx = lambda *_: (0,) * len(shape)
    if single:
        return pl.BlockSpec(shape, idx, pipeline_mode=pl.Buffered(1))
    return pl.BlockSpec(shape, idx)


def _rms(x, g):
    return x * lax.rsqrt(jnp.mean(x * x, axis=-1, keepdims=True) + EPS) * g


def _rms_rows(x_ref, g_ref, xn_ref):
    tm = x_ref.shape[0]
    rc = min(tm, 128)

    def body(r, carry):
        r0 = pl.multiple_of(r * rc, rc)
        xn_ref[pl.ds(r0, rc), :] = _rms(x_ref[pl.ds(r0, rc), :], g_ref[...]).astype(xn_ref.dtype)
        return carry

    lax.fori_loop(0, tm // rc, body, 0)


def _rope(z, c_ref, s1_ref, s2_ref):
    outs = []
    for c in range(z.shape[1] // LANES):
        zc = z[:, c * LANES:(c + 1) * LANES]
        up = pltpu.roll(zc, LANES - ROPE_DIM // 2, axis=1)
        dn = pltpu.roll(zc, ROPE_DIM // 2, axis=1)
        outs.append(zc * c_ref[...] + up * s1_ref[...] + dn * s2_ref[...])
    return jnp.concatenate(outs, axis=1)


def _in_kernel(prompt, q_scale, x_ref, g_ref, w_ref, b_ref, c_ref, s1_ref, s2_ref, *rest):
    if prompt:
        glu_ref, q_ref, k_ref, kt_ref, v_ref, vt_ref, gate_ref, xn_ref = rest
    else:
        glu_ref, q_ref, k_ref, v_ref, gate_ref, xn_ref = rest
    j = pl.program_id(1)
    xn_ref = xn_ref.at[pl.program_id(2)]
    tm, tn = x_ref.shape[0], w_ref.shape[1]
    cw = 2 * LANES
    nchunk = tn // cw

    @pl.when(j == 0)
    def _():
        _rms_rows(x_ref, g_ref, xn_ref)

    def zchunk(c0):
        return jnp.dot(xn_ref[...], w_ref[:, c0:c0 + cw], preferred_element_type=F32)

    @pl.when(j < 2)
    def _():
        half = tn // 2
        for c in range(half // cw):
            a = zchunk(c * cw) + b_ref[:, c * cw:(c + 1) * cw]
            b = zchunk(half + c * cw) + b_ref[:, half + c * cw:half + (c + 1) * cw]
            glu_ref[:, c * cw:(c + 1) * cw] = a * jax.nn.sigmoid(b)

    @pl.when(j == 2)
    def _():
        for c in range(nchunk):
            zr = _rope(zchunk(c * cw), c_ref, s1_ref, s2_ref)
            q_ref[:, c * cw:(c + 1) * cw] = (zr * q_scale).astype(q_ref.dtype)

    @pl.when(j == 3)
    def _():
        for c in range(nchunk):
            zr = _rope(zchunk(c * cw), c_ref, s1_ref, s2_ref)
            k_ref[:, c * cw:(c + 1) * cw] = zr.astype(k_ref.dtype)
            if prompt:
                kt_ref[0, c * cw:(c + 1) * cw, :] = zr.T

    @pl.when(j == 4)
    def _():
        for c in range(nchunk):
            z = zchunk(c * cw)
            v_ref[:, c * cw:(c + 1) * cw] = z
            if prompt:
                kb = vt_ref.shape[4]
                for hh in range(cw // LANES):
                    for blk in range(tm // kb):
                        vt_ref[0, c * (cw // LANES) + hh, blk] = (
                            z[blk * kb:(blk + 1) * kb, hh * LANES:(hh + 1) * LANES].T.astype(BF16))

    @pl.when(j >= 5)
    def _():
        for c in range(nchunk):
            gate_ref[:, c * cw:(c + 1) * cw] = jax.nn.sigmoid(zchunk(c * cw)).astype(gate_ref.dtype)


def _in_proj(x2, g, w_cat, b_cat, tabs, q_scale, tm, seq, prompt):
    n, d = x2.shape
    tn = 1024
    nj = w_cat.shape[1] // tn
    ngate = nj - 5
    nper = tabs[0].shape[0] // tm
    grp = 2 if (n // tm) % 2 == 0 else 1

    def rows_at(j_lo, j_hi):
        def row(io, j, ii):
            return io * grp + jnp.where(j < j_lo, 0, jnp.where(j <= j_hi, ii, grp - 1))
        return row

    x_row = rows_at(0, 0)
    glu_row, rope_row = rows_at(0, 1), rows_at(2, 3)
    q_row, k_row, v_row, gate_row = rows_at(2, 2), rows_at(3, 3), rows_at(4, 4), rows_at(5, nj - 1)
    in_specs = [pl.BlockSpec((tm, d), lambda io, j, ii: (x_row(io, j, ii), 0)), _full((1, d)),
                pl.BlockSpec((d, tn), lambda io, j, ii: (0, j)),
                pl.BlockSpec((1, tn), lambda io, j, ii: (0, jnp.minimum(j, 1)))]
    in_specs += [pl.BlockSpec((tm, LANES), lambda io, j, ii: (rope_row(io, j, ii) % nper, 0))] * 3
    glu_spec = pl.BlockSpec((tm, tn // 2), lambda io, j, ii: (glu_row(io, j, ii), jnp.minimum(j, 1)))
    gate_spec = pl.BlockSpec((tm, tn), lambda io, j, ii: (gate_row(io, j, ii), jnp.clip(j - 5, 0, ngate - 1)))

    def row_spec(row):
        return pl.BlockSpec((tm, tn), lambda io, j, ii: (row(io, j, ii), 0))

    sds = jax.ShapeDtypeStruct
    if prompt:
        tpb = seq // tm
        nb = n // seq

        def t_spec(shape, row):
            def idx(io, j, ii):
                r = row(io, j, ii)
                return (r // tpb, 0, r % tpb) + (0,) * (len(shape) - 3)
            return pl.BlockSpec(shape, idx)

        out_specs = [glu_spec, row_spec(q_row), row_spec(k_row), t_spec((1, tn, tm), k_row), row_spec(v_row),
                     t_spec((1, tn // LANES, tm // KV_BLOCK, LANES, KV_BLOCK), v_row), gate_spec]
        out_shape = [sds((n, tn), F32), sds((n, tn), BF16), sds((n, tn), BF16), sds((nb, tn, seq), F32),
                     sds((n, tn), F32), sds((nb, tn // LANES, seq // KV_BLOCK, LANES, KV_BLOCK), BF16),
                     sds((n, ngate * tn), BF16)]
    else:
        out_specs = [glu_spec, row_spec(q_row), row_spec(k_row), row_spec(v_row), gate_spec]
        out_shape = [sds((n, tn), F32), sds((n, tn), BF16), sds((n, tn), F32), sds((n, tn), F32),
                     sds((n, ngate * tn), BF16)]
    return pl.pallas_call(
        functools.partial(_in_kernel, prompt, q_scale), grid=(n // (tm * grp), nj, grp),
        in_specs=in_specs, out_specs=out_specs, out_shape=out_shape,
        scratch_shapes=[pltpu.VMEM((grp, tm, d), BF16)],
        compiler_params=_cparams(("parallel", "arbitrary", "arbitrary")),
        name="in_proj")(x2, g, w_cat, b_cat, *tabs)


def _rope_tables(pos, rows):
    half = ROPE_DIM // 2
    inv_freq = 1.0 / (ROPE_THETA ** (jnp.arange(0, ROPE_DIM, 2, dtype=F32) / ROPE_DIM))
    ang = pos.astype(F32)[:, None] * inv_freq[None, :]
    cos, sin = jnp.cos(ang), jnp.sin(ang)
    t = pos.shape[0]
    sub = 64
    ones = jnp.ones((t, sub - ROPE_DIM), F32)
    zeros = jnp.zeros((t, sub - ROPE_DIM), F32)
    zh = jnp.zeros((t, half), F32)
    c = jnp.concatenate([cos, cos, ones], axis=1)
    s1 = jnp.concatenate([-sin, zh, zeros], axis=1)
    s2 = jnp.concatenate([zh, sin, zeros], axis=1)
    reps = (rows // t, LANES // sub)
    return tuple(jnp.tile(a, reps) for a in (c, s1, s2))


def _conv_kernel(conv_w, glu_ref, hist_ref, wdw_ref, bdw_ref, lng_ref, lnb_ref, wco_ref, bco_ref,
                 gate_ref, o_ref, xs_ref, acc_ref, c_ref):
    tt = glu_ref.shape[1]
    rows = HIST_PAD + tt
    i = pl.program_id(1)

    @pl.when(i == 0)
    def _():
        xs_ref[0, 0:HIST_PAD, :] = hist_ref[0]

    @pl.when(i > 0)
    def _():
        xs_ref[0, 0:HIST_PAD, :] = xs_ref[0, tt:tt + HIST_PAD, :]

    xs_ref[0, HIST_PAD:rows, :] = glu_ref[0]
    for b in range(1, SUBLANES):
        xs_ref[b, 0:rows - SUBLANES, :] = xs_ref[0, b:b + rows - SUBLANES, :]

    rc = 2 * SUBLANES
    off = HIST_PAD - (conv_w - 1)

    def chunk(r, carry):
        r0 = pl.multiple_of(r * rc, rc)
        accs = [None] * (rc // SUBLANES)
        for j in range(conv_w):
            a, b = divmod(off + j, SUBLANES)
            wj = wdw_ref[j]
            for u in range(len(accs)):
                start = pl.multiple_of(r0 + (a + u) * SUBLANES, SUBLANES)
                term = xs_ref[b, pl.ds(start, SUBLANES), :] * wj
                accs[u] = term if accs[u] is None else accs[u] + term
        acc_ref[pl.ds(r0, rc), :] = jnp.concatenate(accs, axis=0)
        return carry

    lax.fori_loop(0, tt // rc, chunk, 0)
    acc = acc_ref[...] + bdw_ref[...]
    mu = jnp.mean(acc, axis=-1, keepdims=True)
    xc = acc - mu
    var = jnp.mean(xc * xc, axis=-1, keepdims=True)
    y = xc * lax.rsqrt(var + EPS) * lng_ref[...] + lnb_ref[...]
    c_ref[...] = (y * jax.nn.sigmoid(y)).astype(BF16)
    out = jnp.dot(c_ref[...], wco_ref[...], preferred_element_type=F32) + bco_ref[...]
    o_ref[0] = (gate_ref[0].astype(F32) * out).astype(o_ref.dtype)


def _conv_branch(glu3, hist, w_dw, b_dw, ln_g, ln_b, w_co_bf, b_co, gates3, tt):
    b, t, c = glu3.shape
    d = w_co_bf.shape[1]
    conv_w = w_dw.shape[0]
    return pl.pallas_call(
        functools.partial(_conv_kernel, conv_w), grid=(b, t // tt),
        in_specs=[pl.BlockSpec((1, tt, c), lambda bi, i: (bi, i, 0)),
                  pl.BlockSpec((1, HIST_PAD, c), lambda bi, i: (bi, 0, 0)),
                  _full((conv_w, SUBLANES, c)), _full((1, c)), _full((1, c)), _full((1, c)),
                  _full((c, d)), _full((1, d)),
                  pl.BlockSpec((1, tt, d), lambda bi, i: (bi, i, 0))],
        out_specs=pl.BlockSpec((1, tt, d), lambda bi, i: (bi, i, 0)),
        out_shape=jax.ShapeDtypeStruct((b, t, d), BF16),
        scratch_shapes=[pltpu.VMEM((SUBLANES, HIST_PAD + tt, c), F32), pltpu.VMEM((tt, c), F32),
                        pltpu.VMEM((tt, c), BF16)],
        compiler_params=_cparams(("parallel", "arbitrary")),
        name="conv_branch")(glu3, hist, jnp.broadcast_to(w_dw[:, None, :], (conv_w, SUBLANES, c)), b_dw, ln_g, ln_b,
                            w_co_bf, b_co, gates3)


def _lambda(lam_ref, lam_init):
    lq1, lk1, lq2, lk2 = (lam_ref[r:r + 1, :] for r in range(4))
    return (jnp.exp(jnp.sum(lq1 * lk1, axis=-1, keepdims=True))
            - jnp.exp(jnp.sum(lq2 * lk2, axis=-1, keepdims=True)) + lam_init)


def _split_maps(q):
    lane = lax.broadcasted_iota(I32, q.shape, 1)
    zero = jnp.zeros_like(q)
    half = q.shape[1] // 2
    return jnp.where(lane < half, q, zero), jnp.where(lane >= half, q, zero)


def _nt(a, b):
    return lax.dot_general(a, b, (((1,), (1,)), ((), ())), preferred_element_type=F32)


def _attn_prompt_kernel(lam_init, q_ref, k_ref, vt_ref, lam_ref, sgt_ref, o_ref, acc_ref, sa_ref, sb_ref):
    tq = q_ref.shape[1]
    tk = vt_ref.shape[4]
    assert tq == 2 * tk
    qi = pl.program_id(2)
    qz = _split_maps(q_ref[0])
    acc_ref[...] = jnp.zeros_like(acc_ref)
    cdiff = (lax.broadcasted_iota(I32, (tk, tq), 0) // CHUNK) - (lax.broadcasted_iota(I32, (tk, tq), 1) // CHUNK)

    def scores(kb, s_ref):
        k0 = pl.multiple_of(kb * tk, tk)
        kblk = k_ref[0, pl.ds(k0, tk), :]
        for m in range(2):
            s_ref[m] = _nt(kblk, qz[m])

    def consume(kb, s_ref, carry, diag_block):
        vblk = vt_ref[0, 0, kb]
        new, alphas, ps = [], [], []
        for m in range(2):
            mx, l = carry[m]
            if diag_block is None:
                sm = s_ref[m]
            else:
                sm = jnp.where(cdiff <= -diag_block * (tk // CHUNK), s_ref[m], -jnp.inf)
            mn = jnp.maximum(mx, jnp.max(sm, axis=0, keepdims=True))
            alpha = jnp.exp2(mx - mn)
            p = jnp.exp2(sm - mn)
            new.append((mn, alpha * l + jnp.sum(p, axis=0, keepdims=True)))
            alphas.append(alpha)
            ps.append(p.astype(BF16))
        pv = [jnp.dot(vblk, ps[m], preferred_element_type=F32) for m in range(2)]
        for m in range(2):
            acc_ref[m] = alphas[m] * acc_ref[m] + pv[m]
        return tuple(new)

    def pair(j, carry):
        scores(2 * j + 1, sb_ref)
        carry = consume(2 * j, sa_ref, carry, None)
        scores(2 * j + 2, sa_ref)
        return consume(2 * j + 1, sb_ref, carry, None)

    init = tuple((jnp.full((1, tq), -jnp.inf, F32), jnp.zeros((1, tq), F32)) for _ in range(2))
    scores(0, sa_ref)
    carry = lax.fori_loop(0, qi, pair, init)
    scores(2 * qi + 1, sb_ref)
    carry = consume(2 * qi, sa_ref, carry, 0)
    carry = consume(2 * qi + 1, sb_ref, carry, 1)
    lam = _lambda(lam_ref, lam_init)
    ot = acc_ref[0] * (1.0 / carry[0][1]) - lam * (acc_ref[1] * (1.0 / carry[1][1]))
    ms = jnp.mean(ot * ot, axis=0, keepdims=True)
    yt = ot * lax.rsqrt(ms + EPS) * sgt_ref[...] * (1.0 - lam_init)
    o_ref[0] = yt.T.astype(o_ref.dtype)


def _attn_prompt(q3, k3, vt5, lam4, sgt, lam_init, tq):
    b, t, w = q3.shape
    n_heads = vt5.shape[1]
    hw = w // n_heads
    blk = lambda bi, h, i: (bi, i, h)
    return pl.pallas_call(
        functools.partial(_attn_prompt_kernel, lam_init), grid=(b, n_heads, t // tq),
        in_specs=[pl.BlockSpec((1, tq, hw), blk),
                  pl.BlockSpec((1, t, hw), lambda bi, h, i: (bi, 0, h)),
                  pl.BlockSpec((1, 1) + vt5.shape[2:], lambda bi, h, i: (bi, h, 0, 0, 0)),
                  _full(lam4.shape), _full(sgt.shape)],
        out_specs=pl.BlockSpec((1, tq, hw), blk),
        out_shape=jax.ShapeDtypeStruct((b, t, w), BF16),
        scratch_shapes=[pltpu.VMEM((2, vt5.shape[3], tq), F32), pltpu.VMEM((2, KV_BLOCK, tq), F32),
                        pltpu.VMEM((2, KV_BLOCK, tq), F32)],
        compiler_params=_cparams(("parallel", "parallel", "arbitrary")),
        name="attn_prompt")(q3, k3, vt5, lam4, sgt)


def _attn_sample_kernel(lam_init, n_heads, q_ref, ckt_ref, cv_ref, kn_ref, vn_ref, lam_ref, sg_ref, o_ref):
    past = ckt_ref.shape[2]
    hw = q_ref.shape[2] // n_heads
    lam = _lambda(lam_ref, lam_init)
    for h in range(n_heads):
        cols = slice(h * hw, (h + 1) * hw)
        qs = jnp.concatenate(_split_maps(q_ref[0, :, cols]), axis=0)
        t = qs.shape[0] // 2
        ckt = ckt_ref[0, cols, :].astype(BF16)
        cv = cv_ref[0, pl.ds(h, past, stride=n_heads), :].astype(BF16)
        kn = kn_ref[0, :, cols].astype(BF16)
        vn = vn_ref[0, :, cols].astype(BF16)
        sp = jnp.dot(qs, ckt, preferred_element_type=F32)
        sn = _nt(qs, kn)
        mx = jnp.maximum(jnp.max(sp, axis=-1, keepdims=True), jnp.max(sn, axis=-1, keepdims=True))
        pp = jnp.exp2(sp - mx)
        pn = jnp.exp2(sn - mx)
        l = jnp.sum(pp, axis=-1, keepdims=True) + jnp.sum(pn, axis=-1, keepdims=True)
        acc = (jnp.dot(pp.astype(BF16), cv, preferred_element_type=F32)
               + jnp.dot(pn.astype(BF16), vn, preferred_element_type=F32))
        on = acc / l
        o = on[:t] - lam * on[t:]
        y = _rms(o, sg_ref[...]) * (1.0 - lam_init)
        o_ref[0, :, cols] = y.astype(o_ref.dtype)


def _attn_sample(q3, ckt3, cv3, kn3, vn3, lam4, subln_g, lam_init, n_heads):
    b, t, w = q3.shape
    idx = lambda bi: (bi, 0, 0)
    return pl.pallas_call(
        functools.partial(_attn_sample_kernel, lam_init, n_heads), grid=(b,),
        in_specs=[pl.BlockSpec((1, t, w), idx), pl.BlockSpec((1,) + ckt3.shape[1:], idx),
                  pl.BlockSpec((1,) + cv3.shape[1:], idx), pl.BlockSpec((1, t, w), idx),
                  pl.BlockSpec((1, t, w), idx), _full(lam4.shape), _full(subln_g.shape)],
        out_specs=pl.BlockSpec((1, t, w), idx),
        out_shape=jax.ShapeDtypeStruct((b, t, w), BF16),
        compiler_params=_cparams(("parallel",)),
        name="attn_sample")(q3, ckt3, cv3, kn3, vn3, lam4, subln_g)


def _mid_kernel(n_groups, epg, x_ref, gc_ref, ga_ref, o_ref, wao_ref, wout_ref, g2_ref, wr2_ref, wrh_ref,
                br_ref, x1_ref, route_ref):
    attn = jnp.dot(o_ref[...], wao_ref[...], preferred_element_type=F32)
    merged = gc_ref[...].astype(F32) + ga_ref[...].astype(F32) * attn
    x1 = x_ref[...] + jnp.dot(merged.astype(BF16), wout_ref[...], preferred_element_type=F32)
    x1_ref[...] = x1
    xn2 = _rms(x1, g2_ref[...])

    n_exp = n_groups * epg
    hi = xn2.astype(BF16)
    lo = (xn2 - hi.astype(F32)).astype(BF16)
    both = jnp.dot(hi, wr2_ref[...], preferred_element_type=F32)
    lg = (both[:, :LANES] + both[:, LANES:]
          + jnp.dot(lo, wrh_ref[...], preferred_element_type=F32) + br_ref[...])
    lane = lax.broadcasted_iota(I32, lg.shape, 1)
    neg = -jnp.inf
    big = jnp.int32(2 ** 30)
    is_g = (lane >= n_exp) & (lane < n_exp + n_groups)
    gl = jnp.where(is_g, lg, neg)
    gmax = jnp.max(gl, axis=-1, keepdims=True)
    g_idx = jnp.min(jnp.where(gl == gmax, lane, big), axis=-1, keepdims=True) - n_exp
    p_g = 1.0 / jnp.sum(jnp.where(is_g, jnp.exp(gl - gmax), 0.0), axis=-1, keepdims=True)
    in_grp = (lane >= g_idx * epg) & (lane < (g_idx + 1) * epg)
    el = jnp.where(in_grp, lg, neg)
    v1 = jnp.max(el, axis=-1, keepdims=True)
    e1 = jnp.min(jnp.where(el == v1, lane, big), axis=-1, keepdims=True)
    el2 = jnp.where(lane == e1, neg, el)
    v2 = jnp.max(el2, axis=-1, keepdims=True)
    e2 = jnp.min(jnp.where(el2 == v2, lane, big), axis=-1, keepdims=True)
    t2 = jnp.exp(v2 - v1)
    den = 1.0 + t2
    w1 = p_g / den
    w2 = p_g * t2 / den
    route = jnp.where(lane == 0, e1.astype(F32),
                      jnp.where(lane == 1, e2.astype(F32),
                                jnp.where(lane == 2, w1, jnp.where(lane == 3, w2, 0.0))))
    route_ref[...] = route


def _mid(x2, gc2, gates2, o2, w_ao_bf, w_out_bf, g2, wr2, wrh, br, n_groups, epg, tm):
    n, d = x2.shape
    wo = o2.shape[1]
    row = lambda i: (i, 0)
    return pl.pallas_call(
        functools.partial(_mid_kernel, n_groups, epg), grid=(n // tm,),
        in_specs=[pl.BlockSpec((tm, d), row), pl.BlockSpec((tm, d), row),
                  pl.BlockSpec((tm, d), lambda i: (i, 1)), pl.BlockSpec((tm, wo), row),
                  _full(w_ao_bf.shape, True), _full(w_out_bf.shape, True), _full((1, d)),
                  _full(wr2.shape, True), _full(wrh.shape, True), _full(br.shape)],
        out_specs=[pl.BlockSpec((tm, d), row), pl.BlockSpec((tm, LANES), row)],
        out_shape=[jax.ShapeDtypeStruct((n, d), F32), jax.ShapeDtypeStruct((n, LANES), F32)],
        compiler_params=_cparams(("parallel",)),
        name="mid")(x2, gc2, gates2, o2, w_ao_bf, w_out_bf, g2, wr2, wrh, br)


def _dispatch_kernel(nt, slot_ref, x1_ref, g2_ref, xs_in_hbm, xs_hbm, xn_ref, sem):
    del xs_in_hbm
    tmd = x1_ref.shape[0]
    i = pl.program_id(0)
    cur = i % 2
    xn = xn_ref.at[cur]
    xn[...] = _rms(x1_ref[...], g2_ref[...])

    for r in range(tmd):
        for k in range(2):
            pltpu.make_async_copy(xn.at[pl.ds(r, 1)], xs_hbm.at[pl.ds(slot_ref[0, 0, 2 * r + k], 1)],
                                  sem.at[cur]).start(priority=k)

    def drain(half):
        for _ in range(2):
            pltpu.make_async_copy(xn_ref.at[half], xs_hbm.at[pl.ds(0, tmd)], sem.at[half]).wait()

    @pl.when(i > 0)
    def _():
        drain(1 - cur)

    @pl.when(i == nt - 1)
    def _():
        drain(cur)


def _dispatch(slots, x1, g2, xs, tmd):
    n, d = x1.shape
    nt = n // tmd
    return pl.pallas_call(
        functools.partial(_dispatch_kernel, nt), grid=(nt,),
        in_specs=[pl.BlockSpec((1, 1, 2 * tmd), lambda i: (i, 0, 0), memory_space=pltpu.SMEM),
                  pl.BlockSpec((tmd, d), lambda i: (i, 0)), _full((1, d)),
                  pl.BlockSpec(memory_space=pl.ANY)],
        out_specs=pl.BlockSpec(memory_space=pl.ANY),
        out_shape=jax.ShapeDtypeStruct(xs.shape, xs.dtype),
        scratch_shapes=[pltpu.VMEM((2, tmd, d), F32), pltpu.SemaphoreType.DMA((2,))],
        input_output_aliases={3: 0},
        compiler_params=_cparams(("arbitrary",)),
        name="dispatch")(slots.reshape(nt, 1, 2 * tmd), x1, g2, xs)


def _expert_kernel(te_ref, nt_ref, xs_ref, wg_ref, wu_ref, wd_ref, ys_ref, wgb_ref, wub_ref, wdb_ref):
    i = pl.program_id(0)
    used = i < nt_ref[0]
    new_expert = (i == 0) | (te_ref[i] != te_ref[jnp.maximum(i - 1, 0)])

    @pl.when(used & new_expert)
    def _():
        wgb_ref[...] = wg_ref[0].astype(BF16)
        wub_ref[...] = wu_ref[0].astype(BF16)
        wdb_ref[...] = wd_ref[0].astype(BF16)

    @pl.when(used)
    def _():
        x = xs_ref[...].astype(BF16)
        hg = jnp.dot(x, wgb_ref[...], preferred_element_type=F32)
        hu = jnp.dot(x, wub_ref[...], preferred_element_type=F32)
        h = (hg * jax.nn.sigmoid(hg)) * hu
        ys_ref[...] = jnp.dot(h.astype(BF16), wdb_ref[...], preferred_element_type=F32)

    @pl.when(jnp.logical_not(used))
    def _():
        ys_ref[...] = jnp.zeros_like(ys_ref)


def _experts(tile_expert, n_tiles, xs, wg, wu, wd, tme):
    s, d = xs.shape
    de = wg.shape[2]
    tile = lambda i, te, nt: (jnp.minimum(i, nt[0] - 1), 0)
    wmap = lambda i, te, nt: (te[jnp.minimum(i, nt[0] - 1)], 0, 0)
    grid_spec = pltpu.PrefetchScalarGridSpec(
        num_scalar_prefetch=2, grid=(s // tme,),
        in_specs=[pl.BlockSpec((tme, d), tile),
                  pl.BlockSpec((1, d, de), wmap), pl.BlockSpec((1, d, de), wmap), pl.BlockSpec((1, de, d), wmap)],
        out_specs=pl.BlockSpec((tme, d), lambda i, te, nt: (i, 0)),
        scratch_shapes=[pltpu.VMEM((d, de), BF16), pltpu.VMEM((d, de), BF16), pltpu.VMEM((de, d), BF16)])
    return pl.pallas_call(
        _expert_kernel, grid_spec=grid_spec,
        out_shape=jax.ShapeDtypeStruct((s, d), F32),
        compiler_params=_cparams(("arbitrary",)),
        name="experts")(tile_expert, n_tiles, xs, wg, wu, wd)


def _combine_kernel(nt, slot_ref, slot_next_ref, x1_ref, route_ref, gf_ref, ys_hbm, y_ref, buf_ref, sem):
    tmf = x1_ref.shape[0]
    i = pl.program_id(0)
    cur = i % 2

    def gather(s_ref, half):
        for r in range(tmf):
            for k in range(2):
                pltpu.make_async_copy(ys_hbm.at[pl.ds(s_ref[0, 0, 2 * r + k], 1)],
                                      buf_ref.at[half, k, pl.ds(r, 1)], sem.at[half]).start(priority=k)

    @pl.when(i == 0)
    def _():
        gather(slot_ref, 0)

    @pl.when(i + 1 < nt)
    def _():
        gather(slot_next_ref, 1 - cur)

    for k in range(2):
        pltpu.make_async_copy(ys_hbm.at[pl.ds(0, tmf)], buf_ref.at[cur, k], sem.at[cur]).wait()
    w1 = route_ref[:, 2:3]
    w2 = route_ref[:, 3:4]
    x2 = x1_ref[...] + (w1 * buf_ref[cur, 0] + w2 * buf_ref[cur, 1])
    y_ref[...] = _rms(x2, gf_ref[...])


def _combine(slots, x1, route, gf, ys, tmf):
    n, d = x1.shape
    nt = n // tmf
    row = lambda i: (i, 0)
    slots3 = slots.reshape(nt, 1, 2 * tmf)
    return pl.pallas_call(
        functools.partial(_combine_kernel, nt), grid=(nt,),
        in_specs=[pl.BlockSpec((1, 1, 2 * tmf), lambda i: (i, 0, 0), memory_space=pltpu.SMEM),
                  pl.BlockSpec((1, 1, 2 * tmf), lambda i: (jnp.minimum(i + 1, nt - 1), 0, 0),
                               memory_space=pltpu.SMEM),
                  pl.BlockSpec((tmf, d), row), pl.BlockSpec((tmf, LANES), row), _full((1, d)),
                  pl.BlockSpec(memory_space=pl.ANY)],
        out_specs=pl.BlockSpec((tmf, d), row),
        out_shape=jax.ShapeDtypeStruct((n, d), F32),
        scratch_shapes=[pltpu.VMEM((2, 2, tmf, d), F32), pltpu.SemaphoreType.DMA((2,))],
        compiler_params=_cparams(("arbitrary",)),
        name="combine")(slots3, slots3, x1, route, gf, ys)


def _pick(n, pref):
    t = min(n, pref)
    while n % t:
        t //= 2
    return t


def _mixer(x3, hist, k_past_t, v_past, pos, p, n_heads, lam_init):
    b, t, d = x3.shape
    n = b * t
    x2 = x3.reshape(n, d)
    c_conv = p["w_dw"].shape[1]
    prompt = k_past_t is None
    tm = _pick(n, 512)
    tabs = _rope_tables(pos, max(t, tm))
    q_scale = (d // n_heads // 4) ** -0.5 * LOG2E
    outs = _in_proj(x2, p["norm1_g"], p["w_cat"], p["b_cat"], tabs, q_scale, tm, t, prompt)
    if prompt:
        glu, q, k_bf, k_t, v, v_t, gates = outs
    else:
        glu, q, k, v, gates = outs

    glu3 = glu.reshape(b, t, c_conv)
    gates3 = gates.reshape(b, t, 2 * d)
    gc = _conv_branch(glu3, hist, p["w_dw"], p["b_dw"], p["ln_g"], p["ln_b"], p["w_co"], p["b_co"],
                      gates3, _pick(t, 512))

    w = q.shape[1]
    q3 = q.reshape(b, t, w)
    hd = w // n_heads // 2
    if prompt:
        o = _attn_prompt(q3, k_bf.reshape(b, t, w), v_t, p["lam4"], p["subln_g"].reshape(-1, 1), lam_init,
                         2 * KV_BLOCK)
        k_out = k_t.reshape(b, n_heads, 2, hd, t).transpose(0, 4, 1, 2, 3)[None]
    else:
        k3 = k.reshape(b, t, w)
        o = _attn_sample(q3, k_past_t, v_past, k3, v.reshape(b, t, w), p["lam4"], p["subln_g"], lam_init, n_heads)
        k_out = k3.reshape(1, b, t, n_heads, 2, hd)
    v_out = v.reshape(1, b, t, n_heads, w // n_heads)

    x1, route = _mid(x2, gc.reshape(n, d), gates, o.reshape(n, w), p["w_ao"], p["w_out"],
                     p["norm2_g"], p["wr2"], p["wrh"], p["br"], p["n_groups"], p["epg"], _pick(n, 512))
    new_hist = glu3[:, t - (p["w_dw"].shape[0] - 1):, :]
    return x1, route, k_out, v_out, new_hist[None]


def kernel(x_prompt, x_sample, cache_k, cache_v, state_conv, norm1_g, w_in, b_glu, w_dw, b_dw, conv_ln_g, conv_ln_b, w_conv_out, b_conv_out, lambda_q1, lambda_k1, lambda_q2, lambda_k2, subln_g, w_attn_out, w_out, norm2_g, w_router_group, b_router_group, w_router_expert, b_router_expert, w_exp_gate, w_exp_up, w_exp_down, final_norm_g):
    bp, tp, d = x_prompt.shape
    bs, ts, _ = x_sample.shape
    assert w_in.shape[0] == 1
    l = 0
    past = cache_k.shape[2]
    n_heads = cache_k.shape[3]
    hd = cache_k.shape[5]
    v_dim = cache_v.shape[4]
    conv_w = w_dw.shape[1]
    c_conv = w_dw.shape[2]
    n_groups = w_router_group.shape[2]
    n_exp = w_router_expert.shape[2]
    epg = n_exp // n_groups
    lam_init = 0.8 - 0.6 * math.exp(-0.3 * l)
    assert c_conv == 1024 and n_heads * 2 * hd == 1024 and n_heads * v_dim == 1024 and tp % (2 * KV_BLOCK) == 0

    hc = c_conv // 2
    wl = w_in[l]
    w_cat = jnp.concatenate([wl[:, :hc], wl[:, c_conv:c_conv + hc], wl[:, hc:c_conv], wl[:, c_conv + hc:]],
                            axis=1).astype(BF16)
    bg = b_glu[l]
    b_cat = jnp.concatenate([bg[:hc], bg[c_conv:c_conv + hc], bg[hc:c_conv], bg[c_conv + hc:]])[None]

    wr = jnp.zeros((d, LANES), F32)
    wr = wr.at[:, :n_exp].set(w_router_expert[l]).at[:, n_exp:n_exp + n_groups].set(w_router_group[l])
    wrh = wr.astype(BF16)
    wrl = (wr - wrh.astype(F32)).astype(BF16)
    br = jnp.zeros((1, LANES), F32)
    br = br.at[0, :n_exp].set(b_router_expert[l]).at[0, n_exp:n_exp + n_groups].set(b_router_group[l])
    p = dict(
        norm1_g=norm1_g[l][None], w_cat=w_cat, b_cat=b_cat,
        w_dw=w_dw[l], b_dw=b_dw[l][None], ln_g=conv_ln_g[l][None], ln_b=conv_ln_b[l][None],
        w_co=w_conv_out[l].astype(BF16), b_co=b_conv_out[l][None],
        lam4=jnp.stack([lambda_q1[l], lambda_k1[l], lambda_q2[l], lambda_k2[l]]),
        subln_g=subln_g[l][None], w_ao=w_attn_out[l].astype(BF16), w_out=w_out[l].astype(BF16),
        norm2_g=norm2_g[l][None], wr2=jnp.concatenate([wrh, wrl], axis=1), wrh=wrh, br=br,
        n_groups=n_groups, epg=epg)

    hist_p = jnp.zeros((bp, HIST_PAD, c_conv), F32)
    hist_s = jnp.concatenate(
        [jnp.zeros((bs, HIST_PAD - (conv_w - 1), c_conv), F32), state_conv[l]], axis=1)
    ckt = cache_k[l].transpose(0, 2, 3, 4, 1).reshape(bs, n_heads * 2 * hd, past)
    cv = cache_v[l].reshape(bs, past * n_heads, v_dim)

    x1p, routep, kp, vp, cp = _mixer(x_prompt, hist_p, None, None, jnp.arange(tp), p, n_heads, lam_init)
    x1s, routes, ks, vs, cs = _mixer(x_sample, hist_s, ckt, cv, past + jnp.arange(ts), p, n_heads, lam_init)

    np_, ns_ = bp * tp, bs * ts
    n = np_ + ns_
    tme = 256
    e_idx = jnp.concatenate([routep[:, :2], routes[:, :2]], axis=0).astype(I32)
    onehot = (e_idx[:, :, None] == jnp.arange(n_exp, dtype=I32)[None, None, :]).astype(I32)
    per_tok = onehot.sum(axis=1)
    before = jnp.cumsum(per_tok, axis=0) - per_tok
    rank = jnp.sum(onehot * before[:, None, :], axis=-1)
    counts = per_tok.sum(axis=0)
    tiles_per = (counts + tme - 1) // tme
    tile_end = jnp.cumsum(tiles_per)
    tile_start = tile_end - tiles_per
    slots = (jnp.sum(onehot * (tile_start * tme)[None, None, :], axis=-1) + rank).astype(I32)
    max_tiles = (2 * n) // tme + n_exp
    tile_ids = jnp.arange(max_tiles, dtype=I32)
    tile_expert = jnp.minimum(jnp.sum(tile_ids[:, None] >= tile_end[None, :], axis=1), n_exp - 1).astype(I32)
    n_tiles = tile_end[-1:].astype(I32)

    g2 = norm2_g[l][None]
    xs = jnp.zeros((max_tiles * tme, d), F32)
    xs = _dispatch(slots[:np_].reshape(-1), x1p, g2, xs, _pick(np_, 256))
    xs = _dispatch(slots[np_:].reshape(-1), x1s, g2, xs, _pick(ns_, 256))
    ys = _experts(tile_expert, n_tiles, xs, w_exp_gate[l], w_exp_up[l], w_exp_down[l], tme)
    gf = final_norm_g[None]
    yp = _combine(slots[:np_].reshape(-1), x1p, routep, gf, ys, _pick(np_, 256))
    ysm = _combine(slots[np_:].reshape(-1), x1s, routes, gf, ys, _pick(ns_, 256))

    return (yp.reshape(bp, tp, d), ysm.reshape(bs, ts, d), kp, vp, cp, ks, vs, cs)
```

```python
import functools
import math

import jax
import jax.numpy as jnp
from jax import lax
from jax.experimental import pallas as pl
from jax.experimental.pallas import tpu as pltpu

F32 = jnp.float32
BF16 = jnp.bfloat16
I32 = jnp.int32

EPS = 1e-6
CHUNK = 64
ROPE_DIM = 16
ROPE_THETA = 500000.0
LOG2E = 1.4426950408889634
LANES = 128
SUBLANES = 8
KV_BLOCK = 256
HIST_PAD = 32
VMEM_LIMIT = 56 * 1024 * 1024


def _cparams(sem):
    return pltpu.CompilerParams(dimension_semantics=sem, vmem_limit_bytes=VMEM_LIMIT)


def _full(shape, single=False):
    idx = lambda *_: (0,) * len(shape)
    if single:
        return pl.BlockSpec(shape, idx, pipeline_mode=pl.Buffered(1))
    return pl.BlockSpec(shape, idx)


def _rms(x, g):
    return x * lax.rsqrt(jnp.mean(x * x, axis=-1, keepdims=True) + EPS) * g


def _rms_rows(x_ref, g_ref, xn_ref):
    tm = x_ref.shape[0]
    rc = min(tm, 128)

    def body(r, carry):
        r0 = pl.multiple_of(r * rc, rc)
        xn_ref[pl.ds(r0, rc), :] = _rms(x_ref[pl.ds(r0, rc), :], g_ref[...]).astype(xn_ref.dtype)
        return carry

    lax.fori_loop(0, tm // rc, body, 0)


def _rope(z, c_ref, s1_ref, s2_ref):
    outs = []
    for c in range(z.shape[1] // LANES):
        zc = z[:, c * LANES:(c + 1) * LANES]
        up = pltpu.roll(zc, LANES - ROPE_DIM // 2, axis=1)
        dn = pltpu.roll(zc, ROPE_DIM // 2, axis=1)
        outs.append(zc * c_ref[...] + up * s1_ref[...] + dn * s2_ref[...])
    return jnp.concatenate(outs, axis=1)


def _in_kernel(prompt, q_scale, x_ref, g_ref, w_ref, b_ref, c_ref, s1_ref, s2_ref, *rest):
    if prompt:
        glu_ref, q_ref, k_ref, kt_ref, v_ref, vt_ref, gate_ref, xn_ref = rest
    else:
        glu_ref, q_ref, k_ref, v_ref, gate_ref, xn_ref = rest
    j = pl.program_id(1)
    xn_ref = xn_ref.at[pl.program_id(2)]
    tm, tn = x_ref.shape[0], w_ref.shape[1]
    cw = 2 * LANES
    nchunk = tn // cw

    @pl.when(j == 0)
    def _():
        _rms_rows(x_ref, g_ref, xn_ref)

    def zchunk(c0):
        return jnp.dot(xn_ref[...], w_ref[:, c0:c0 + cw], preferred_element_type=F32)

    @pl.when(j < 2)
    def _():
        half = tn // 2
        for c in range(half // cw):
            a = zchunk(c * cw) + b_ref[:, c * cw:(c + 1) * cw]
            b = zchunk(half + c * cw) + b_ref[:, half + c * cw:half + (c + 1) * cw]
            glu_ref[:, c * cw:(c + 1) * cw] = a * jax.nn.sigmoid(b)

    @pl.when(j == 2)
    def _():
        for c in range(nchunk):
            zr = _rope(zchunk(c * cw), c_ref, s1_ref, s2_ref)
            q_ref[:, c * cw:(c + 1) * cw] = (zr * q_scale).astype(q_ref.dtype)

    @pl.when(j == 3)
    def _():
        for c in range(nchunk):
            zr = _rope(zchunk(c * cw), c_ref, s1_ref, s2_ref)
            k_ref[:, c * cw:(c + 1) * cw] = zr.astype(k_ref.dtype)
            if prompt:
                kt_ref[0, c * cw:(c + 1) * cw, :] = zr.T

    @pl.when(j == 4)
    def _():
        for c in range(nchunk):
            z = zchunk(c * cw)
            v_ref[:, c * cw:(c + 1) * cw] = z
            if prompt:
                kb = vt_ref.shape[4]
                for hh in range(cw // LANES):
                    for blk in range(tm // kb):
                        vt_ref[0, c * (cw // LANES) + hh, blk] = (
                            z[blk * kb:(blk + 1) * kb, hh * LANES:(hh + 1) * LANES].T.astype(BF16))

    @pl.when(j >= 5)
    def _():
        for c in range(nchunk):
            gate_ref[:, c * cw:(c + 1) * cw] = jax.nn.sigmoid(zchunk(c * cw)).astype(gate_ref.dtype)


def _in_proj(x2, g, w_cat, b_cat, tabs, q_scale, tm, seq, prompt):
    n, d = x2.shape
    tn = 1024
    nj = w_cat.shape[1] // tn
    ngate = nj - 5
    nper = tabs[0].shape[0] // tm
    grp = 2 if (n // tm) % 2 == 0 else 1

    def rows_at(j_lo, j_hi):
        def row(io, j, ii):
            return io * grp + jnp.where(j < j_lo, 0, jnp.where(j <= j_hi, ii, grp - 1))
        return row

    x_row = rows_at(0, 0)
    glu_row, rope_row = rows_at(0, 1), rows_at(2, 3)
    q_row, k_row, v_row, gate_row = rows_at(2, 2), rows_at(3, 3), rows_at(4, 4), rows_at(5, nj - 1)
    in_specs = [pl.BlockSpec((tm, d), lambda io, j, ii: (x_row(io, j, ii), 0)), _full((1, d)),
                pl.BlockSpec((d, tn), lambda io, j, ii: (0, j)),
                pl.BlockSpec((1, tn), lambda io, j, ii: (0, jnp.minimum(j, 1)))]
    in_specs += [pl.BlockSpec((tm, LANES), lambda io, j, ii: (rope_row(io, j, ii) % nper, 0))] * 3
    glu_spec = pl.BlockSpec((tm, tn // 2), lambda io, j, ii: (glu_row(io, j, ii), jnp.minimum(j, 1)))
    gate_spec = pl.BlockSpec((tm, tn), lambda io, j, ii: (gate_row(io, j, ii), jnp.clip(j - 5, 0, ngate - 1)))

    def row_spec(row):
        return pl.BlockSpec((tm, tn), lambda io, j, ii: (row(io, j, ii), 0))

    sds = jax.ShapeDtypeStruct
    if prompt:
        tpb = seq // tm
        nb = n // seq

        def t_spec(shape, row):
            def idx(io, j, ii):
                r = row(io, j, ii)
                return (r // tpb, 0, r % tpb) + (0,) * (len(shape) - 3)
            return pl.BlockSpec(shape, idx)

        out_specs = [glu_spec, row_spec(q_row), row_spec(k_row), t_spec((1, tn, tm), k_row), row_spec(v_row),
                     t_spec((1, tn // LANES, tm // KV_BLOCK, LANES, KV_BLOCK), v_row), gate_spec]
        out_shape = [sds((n, tn), F32), sds((n, tn), BF16), sds((n, tn), BF16), sds((nb, tn, seq), F32),
                     sds((n, tn), F32), sds((nb, tn // LANES, seq // KV_BLOCK, LANES, KV_BLOCK), BF16),
                     sds((n, ngate * tn), BF16)]
    else:
        out_specs = [glu_spec, row_spec(q_row), row_spec(k_row), row_spec(v_row), gate_spec]
        out_shape = [sds((n, tn), F32), sds((n, tn), BF16), sds((n, tn), F32), sds((n, tn), F32),
                     sds((n, ngate * tn), BF16)]
    return pl.pallas_call(
        functools.partial(_in_kernel, prompt, q_scale), grid=(n // (tm * grp), nj, grp),
        in_specs=in_specs, out_specs=out_specs, out_shape=out_shape,
        scratch_shapes=[pltpu.VMEM((grp, tm, d), BF16)],
        compiler_params=_cparams(("parallel", "arbitrary", "arbitrary")),
        name="in_proj")(x2, g, w_cat, b_cat, *tabs)


def _rope_tables(pos, rows):
    half = ROPE_DIM // 2
    inv_freq = 1.0 / (ROPE_THETA ** (jnp.arange(0, ROPE_DIM, 2, dtype=F32) / ROPE_DIM))
    ang = pos.astype(F32)[:, None] * inv_freq[None, :]
    cos, sin = jnp.cos(ang), jnp.sin(ang)
    t = pos.shape[0]
    sub = 64
    ones = jnp.ones((t, sub - ROPE_DIM), F32)
    zeros = jnp.zeros((t, sub - ROPE_DIM), F32)
    zh = jnp.zeros((t, half), F32)
    c = jnp.concatenate([cos, cos, ones], axis=1)
    s1 = jnp.concatenate([-sin, zh, zeros], axis=1)
    s2 = jnp.concatenate([zh, sin, zeros], axis=1)
    reps = (rows // t, LANES // sub)
    return tuple(jnp.tile(a, reps) for a in (c, s1, s2))


def _conv_kernel(conv_w, glu_ref, hist_ref, wdw_ref, bdw_ref, lng_ref, lnb_ref, wco_ref, bco_ref,
                 gate_ref, o_ref, xs_ref, acc_ref, c_ref):
    tt = glu_ref.shape[1]
    rows = HIST_PAD + tt
    i = pl.program_id(1)

    @pl.when(i == 0)
    def _():
        xs_ref[0, 0:HIST_PAD, :] = hist_ref[0]

    @pl.when(i > 0)
    def _():
        xs_ref[0, 0:HIST_PAD, :] = xs_ref[0, tt:tt + HIST_PAD, :]

    xs_ref[0, HIST_PAD:rows, :] = glu_ref[0]
    for b in range(1, SUBLANES):
        xs_ref[b, 0:rows - SUBLANES, :] = xs_ref[0, b:b + rows - SUBLANES, :]

    rc = 2 * SUBLANES
    off = HIST_PAD - (conv_w - 1)

    def chunk(r, carry):
        r0 = pl.multiple_of(r * rc, rc)
        accs = [None] * (rc // SUBLANES)
        for j in range(conv_w):
            a, b = divmod(off + j, SUBLANES)
            wj = wdw_ref[j]
            for u in range(len(accs)):
                start = pl.multiple_of(r0 + (a + u) * SUBLANES, SUBLANES)
                term = xs_ref[b, pl.ds(start, SUBLANES), :] * wj
                accs[u] = term if accs[u] is None else accs[u] + term
        acc_ref[pl.ds(r0, rc), :] = jnp.concatenate(accs, axis=0)
        return carry

    lax.fori_loop(0, tt // rc, chunk, 0)
    acc = acc_ref[...] + bdw_ref[...]
    mu = jnp.mean(acc, axis=-1, keepdims=True)
    xc = acc - mu
    var = jnp.mean(xc * xc, axis=-1, keepdims=True)
    y = xc * lax.rsqrt(var + EPS) * lng_ref[...] + lnb_ref[...]
    c_ref[...] = (y * jax.nn.sigmoid(y)).astype(BF16)
    out = jnp.dot(c_ref[...], wco_ref[...], preferred_element_type=F32) + bco_ref[...]
    o_ref[0] = (gate_ref[0].astype(F32) * out).astype(o_ref.dtype)


def _conv_branch(glu3, hist, w_dw, b_dw, ln_g, ln_b, w_co_bf, b_co, gates3, tt):
    b, t, c = glu3.shape
    d = w_co_bf.shape[1]
    conv_w = w_dw.shape[0]
    return pl.pallas_call(
        functools.partial(_conv_kernel, conv_w), grid=(b, t // tt),
        in_specs=[pl.BlockSpec((1, tt, c), lambda bi, i: (bi, i, 0)),
                  pl.BlockSpec((1, HIST_PAD, c), lambda bi, i: (bi, 0, 0)),
                  _full((conv_w, SUBLANES, c)), _full((1, c)), _full((1, c)), _full((1, c)),
                  _full((c, d)), _full((1, d)),
                  pl.BlockSpec((1, tt, d), lambda bi, i: (bi, i, 0))],
        out_specs=pl.BlockSpec((1, tt, d), lambda bi, i: (bi, i, 0)),
        out_shape=jax.ShapeDtypeStruct((b, t, d), BF16),
        scratch_shapes=[pltpu.VMEM((SUBLANES, HIST_PAD + tt, c), F32), pltpu.VMEM((tt, c), F32),
                        pltpu.VMEM((tt, c), BF16)],
        compiler_params=_cparams(("parallel", "arbitrary")),
        name="conv_branch")(glu3, hist, jnp.broadcast_to(w_dw[:, None, :], (conv_w, SUBLANES, c)), b_dw, ln_g, ln_b,
                            w_co_bf, b_co, gates3)


def _lambda(lam_ref, lam_init):
    lq1, lk1, lq2, lk2 = (lam_ref[r:r + 1, :] for r in range(4))
    return (jnp.exp(jnp.sum(lq1 * lk1, axis=-1, keepdims=True))
            - jnp.exp(jnp.sum(lq2 * lk2, axis=-1, keepdims=True)) + lam_init)


def _split_maps(q):
    lane = lax.broadcasted_iota(I32, q.shape, 1)
    zero = jnp.zeros_like(q)
    half = q.shape[1] // 2
    return jnp.where(lane < half, q, zero), jnp.where(lane >= half, q, zero)


def _nt(a, b):
    return lax.dot_general(a, b, (((1,), (1,)), ((), ())), preferred_element_type=F32)


def _attn_prompt_kernel(lam_init, q_ref, k_ref, vt_ref, lam_ref, sgt_ref, o_ref, *rest):
    if len(rest) == 4:
        rest[0][...] = jnp.zeros_like(rest[0])
    acc_ref, sa_ref, sb_ref = rest[-3:]
    tq = q_ref.shape[1]
    tk = vt_ref.shape[4]
    assert tq == 2 * tk
    qi = pl.program_id(2)
    qz = _split_maps(q_ref[0])
    acc_ref[...] = jnp.zeros_like(acc_ref)
    cdiff = (lax.broadcasted_iota(I32, (tk, tq), 0) // CHUNK) - (lax.broadcasted_iota(I32, (tk, tq), 1) // CHUNK)

    def scores(kb, s_ref):
        k0 = pl.multiple_of(kb * tk, tk)
        kblk = k_ref[0, pl.ds(k0, tk), :]
        for m in range(2):
            s_ref[m] = _nt(kblk, qz[m])

    def consume(kb, s_ref, carry, diag_block):
        vblk = vt_ref[0, 0, kb]
        new, alphas, ps = [], [], []
        for m in range(2):
            mx, l = carry[m]
            if diag_block is None:
                sm = s_ref[m]
            else:
                sm = jnp.where(cdiff <= -diag_block * (tk // CHUNK), s_ref[m], -jnp.inf)
            mn = jnp.maximum(mx, jnp.max(sm, axis=0, keepdims=True))
            alpha = jnp.exp2(mx - mn)
            p = jnp.exp2(sm - mn)
            new.append((mn, alpha * l + jnp.sum(p, axis=0, keepdims=True)))
            alphas.append(alpha)
            ps.append(p.astype(BF16))
        pv = [jnp.dot(vblk, ps[m], preferred_element_type=F32) for m in range(2)]
        for m in range(2):
            acc_ref[m] = alphas[m] * acc_ref[m] + pv[m]
        return tuple(new)

    def pair(j, carry):
        scores(2 * j + 1, sb_ref)
        carry = consume(2 * j, sa_ref, carry, None)
        scores(2 * j + 2, sa_ref)
        return consume(2 * j + 1, sb_ref, carry, None)

    init = tuple((jnp.full((1, tq), -jnp.inf, F32), jnp.zeros((1, tq), F32)) for _ in range(2))
    scores(0, sa_ref)
    carry = lax.fori_loop(0, qi, pair, init)
    scores(2 * qi + 1, sb_ref)
    carry = consume(2 * qi, sa_ref, carry, 0)
    carry = consume(2 * qi + 1, sb_ref, carry, 1)
    lam = _lambda(lam_ref, lam_init)
    ot = acc_ref[0] * (1.0 / carry[0][1]) - lam * (acc_ref[1] * (1.0 / carry[1][1]))
    ms = jnp.mean(ot * ot, axis=0, keepdims=True)
    yt = ot * lax.rsqrt(ms + EPS) * sgt_ref[...] * (1.0 - lam_init)
    o_ref[0] = yt.T.astype(o_ref.dtype)


def _attn_prompt(q3, k3, vt5, lam4, sgt, lam_init, tq, zero_shape=None):
    b, t, w = q3.shape
    n_heads = vt5.shape[1]
    hw = w // n_heads
    nq = t // tq
    blk = lambda bi, h, i: (bi, i, h)
    out_specs = [pl.BlockSpec((1, tq, hw), blk)]
    out_shape = [jax.ShapeDtypeStruct((b, t, w), BF16)]
    if zero_shape is not None:
        rows_per = zero_shape[0] // (b * n_heads * nq)
        out_specs.append(pl.BlockSpec((rows_per, zero_shape[1]), lambda bi, h, i: ((bi * n_heads + h) * nq + i, 0)))
        out_shape.append(jax.ShapeDtypeStruct(zero_shape, F32))
    return pl.pallas_call(
        functools.partial(_attn_prompt_kernel, lam_init), grid=(b, n_heads, nq),
        in_specs=[pl.BlockSpec((1, tq, hw), blk),
                  pl.BlockSpec((1, t, hw), lambda bi, h, i: (bi, 0, h)),
                  pl.BlockSpec((1, 1) + vt5.shape[2:], lambda bi, h, i: (bi, h, 0, 0, 0)),
                  _full(lam4.shape), _full(sgt.shape)],
        out_specs=out_specs, out_shape=out_shape,
        scratch_shapes=[pltpu.VMEM((2, vt5.shape[3], tq), F32), pltpu.VMEM((2, KV_BLOCK, tq), F32),
                        pltpu.VMEM((2, KV_BLOCK, tq), F32)],
        compiler_params=_cparams(("parallel", "parallel", "arbitrary")),
        name="attn_prompt")(q3, k3, vt5, lam4, sgt)


def _attn_sample_kernel(lam_init, n_heads, q_ref, ckt_ref, cv_ref, kn_ref, vn_ref, lam_ref, sg_ref, o_ref):
    past = ckt_ref.shape[2]
    hw = q_ref.shape[2] // n_heads
    lam = _lambda(lam_ref, lam_init)
    for h in range(n_heads):
        cols = slice(h * hw, (h + 1) * hw)
        qs = jnp.concatenate(_split_maps(q_ref[0, :, cols]), axis=0)
        t = qs.shape[0] // 2
        ckt = ckt_ref[0, cols, :].astype(BF16)
        cv = cv_ref[0, pl.ds(h, past, stride=n_heads), :].astype(BF16)
        kn = kn_ref[0, :, cols].astype(BF16)
        vn = vn_ref[0, :, cols].astype(BF16)
        sp = jnp.dot(qs, ckt, preferred_element_type=F32)
        sn = _nt(qs, kn)
        mx = jnp.maximum(jnp.max(sp, axis=-1, keepdims=True), jnp.max(sn, axis=-1, keepdims=True))
        pp = jnp.exp2(sp - mx)
        pn = jnp.exp2(sn - mx)
        l = jnp.sum(pp, axis=-1, keepdims=True) + jnp.sum(pn, axis=-1, keepdims=True)
        acc = (jnp.dot(pp.astype(BF16), cv, preferred_element_type=F32)
               + jnp.dot(pn.astype(BF16), vn, preferred_element_type=F32))
        on = acc / l
        o = on[:t] - lam * on[t:]
        y = _rms(o, sg_ref[...]) * (1.0 - lam_init)
        o_ref[0, :, cols] = y.astype(o_ref.dtype)


def _attn_sample(q3, ckt3, cv3, kn3, vn3, lam4, subln_g, lam_init, n_heads):
    b, t, w = q3.shape
    idx = lambda bi: (bi, 0, 0)
    return pl.pallas_call(
        functools.partial(_attn_sample_kernel, lam_init, n_heads), grid=(b,),
        in_specs=[pl.BlockSpec((1, t, w), idx), pl.BlockSpec((1,) + ckt3.shape[1:], idx),
                  pl.BlockSpec((1,) + cv3.shape[1:], idx), pl.BlockSpec((1, t, w), idx),
                  pl.BlockSpec((1, t, w), idx), _full(lam4.shape), _full(subln_g.shape)],
        out_specs=pl.BlockSpec((1, t, w), idx),
        out_shape=jax.ShapeDtypeStruct((b, t, w), BF16),
        compiler_params=_cparams(("parallel",)),
        name="attn_sample")(q3, ckt3, cv3, kn3, vn3, lam4, subln_g)


def _mid_kernel(n_groups, epg, x_ref, gc_ref, ga_ref, o_ref, wao_ref, wout_ref, g2_ref, wr2_ref, wrh_ref,
                br_ref, x1_ref, route_ref):
    attn = jnp.dot(o_ref[...], wao_ref[...], preferred_element_type=F32)
    merged = gc_ref[...].astype(F32) + ga_ref[...].astype(F32) * attn
    x1 = x_ref[...] + jnp.dot(merged.astype(BF16), wout_ref[...], preferred_element_type=F32)
    x1_ref[...] = x1
    xn2 = _rms(x1, g2_ref[...])

    n_exp = n_groups * epg
    hi = xn2.astype(BF16)
    lo = (xn2 - hi.astype(F32)).astype(BF16)
    both = jnp.dot(hi, wr2_ref[...], preferred_element_type=F32)
    lg = (both[:, :LANES] + both[:, LANES:]
          + jnp.dot(lo, wrh_ref[...], preferred_element_type=F32) + br_ref[...])
    lane = lax.broadcasted_iota(I32, lg.shape, 1)
    neg = -jnp.inf
    big = jnp.int32(2 ** 30)
    is_g = (lane >= n_exp) & (lane < n_exp + n_groups)
    gl = jnp.where(is_g, lg, neg)
    gmax = jnp.max(gl, axis=-1, keepdims=True)
    g_idx = jnp.min(jnp.where(gl == gmax, lane, big), axis=-1, keepdims=True) - n_exp
    p_g = 1.0 / jnp.sum(jnp.where(is_g, jnp.exp(gl - gmax), 0.0), axis=-1, keepdims=True)
    in_grp = (lane >= g_idx * epg) & (lane < (g_idx + 1) * epg)
    el = jnp.where(in_grp, lg, neg)
    v1 = jnp.max(el, axis=-1, keepdims=True)
    e1 = jnp.min(jnp.where(el == v1, lane, big), axis=-1, keepdims=True)
    el2 = jnp.where(lane == e1, neg, el)
    v2 = jnp.max(el2, axis=-1, keepdims=True)
    e2 = jnp.min(jnp.where(el2 == v2, lane, big), axis=-1, keepdims=True)
    t2 = jnp.exp(v2 - v1)
    den = 1.0 + t2
    w1 = p_g / den
    w2 = p_g * t2 / den
    route = jnp.where(lane == 0, e1.astype(F32),
                      jnp.where(lane == 1, e2.astype(F32),
                                jnp.where(lane == 2, w1, jnp.where(lane == 3, w2, 0.0))))
    route_ref[...] = route


def _mid(x2, gc2, gates2, o2, w_ao_bf, w_out_bf, g2, wr2, wrh, br, n_groups, epg, tm):
    n, d = x2.shape
    wo = o2.shape[1]
    row = lambda i: (i, 0)
    return pl.pallas_call(
        functools.partial(_mid_kernel, n_groups, epg), grid=(n // tm,),
        in_specs=[pl.BlockSpec((tm, d), row), pl.BlockSpec((tm, d), row),
                  pl.BlockSpec((tm, d), lambda i: (i, 1)), pl.BlockSpec((tm, wo), row),
                  _full(w_ao_bf.shape, True), _full(w_out_bf.shape, True), _full((1, d)),
                  _full(wr2.shape, True), _full(wrh.shape, True), _full(br.shape)],
        out_specs=[pl.BlockSpec((tm, d), row), pl.BlockSpec((tm, LANES), row)],
        out_shape=[jax.ShapeDtypeStruct((n, d), F32), jax.ShapeDtypeStruct((n, LANES), F32)],
        compiler_params=_cparams(("parallel",)),
        name="mid")(x2, gc2, gates2, o2, w_ao_bf, w_out_bf, g2, wr2, wrh, br)


def _dispatch_kernel(nt, slot_ref, x1_ref, g2_ref, xs_in_hbm, xs_hbm, xn_ref, sem):
    del xs_in_hbm
    tmd = x1_ref.shape[0]
    i = pl.program_id(0)
    cur = i % 2
    xn = xn_ref.at[cur]
    xn[...] = _rms(x1_ref[...], g2_ref[...])

    for r in range(tmd):
        for k in range(2):
            pltpu.make_async_copy(xn.at[pl.ds(r, 1)], xs_hbm.at[pl.ds(slot_ref[0, 0, 2 * r + k], 1)],
                                  sem.at[cur]).start(priority=k)

    def drain(half):
        for _ in range(2):
            pltpu.make_async_copy(xn_ref.at[half], xs_hbm.at[pl.ds(0, tmd)], sem.at[half]).wait()

    @pl.when(i > 0)
    def _():
        drain(1 - cur)

    @pl.when(i == nt - 1)
    def _():
        drain(cur)


def _dispatch(slots, x1, g2, xs, tmd):
    n, d = x1.shape
    nt = n // tmd
    return pl.pallas_call(
        functools.partial(_dispatch_kernel, nt), grid=(nt,),
        in_specs=[pl.BlockSpec((1, 1, 2 * tmd), lambda i: (i, 0, 0), memory_space=pltpu.SMEM),
                  pl.BlockSpec((tmd, d), lambda i: (i, 0)), _full((1, d)),
                  pl.BlockSpec(memory_space=pl.ANY)],
        out_specs=pl.BlockSpec(memory_space=pl.ANY),
        out_shape=jax.ShapeDtypeStruct(xs.shape, xs.dtype),
        scratch_shapes=[pltpu.VMEM((2, tmd, d), F32), pltpu.SemaphoreType.DMA((2,))],
        input_output_aliases={3: 0},
        compiler_params=_cparams(("arbitrary",)),
        name="dispatch")(slots.reshape(nt, 1, 2 * tmd), x1, g2, xs)


def _expert_kernel(te_ref, nt_ref, xs_ref, wg_ref, wu_ref, wd_ref, ys_ref, wgb_ref, wub_ref, wdb_ref):
    i = pl.program_id(0)
    used = i < nt_ref[0]
    new_expert = (i == 0) | (te_ref[i] != te_ref[jnp.maximum(i - 1, 0)])

    @pl.when(used & new_expert)
    def _():
        wgb_ref[...] = wg_ref[0].astype(BF16)
        wub_ref[...] = wu_ref[0].astype(BF16)
        wdb_ref[...] = wd_ref[0].astype(BF16)

    @pl.when(used)
    def _():
        x = xs_ref[...].astype(BF16)
        hg = jnp.dot(x, wgb_ref[...], preferred_element_type=F32)
        hu = jnp.dot(x, wub_ref[...], preferred_element_type=F32)
        h = (hg * jax.nn.sigmoid(hg)) * hu
        ys_ref[...] = jnp.dot(h.astype(BF16), wdb_ref[...], preferred_element_type=F32)

    @pl.when(jnp.logical_not(used))
    def _():
        ys_ref[...] = jnp.zeros_like(ys_ref)


def _experts(tile_expert, n_tiles, xs, wg, wu, wd, tme):
    s, d = xs.shape
    de = wg.shape[2]
    tile = lambda i, te, nt: (jnp.minimum(i, nt[0] - 1), 0)
    wmap = lambda i, te, nt: (te[jnp.minimum(i, nt[0] - 1)], 0, 0)
    grid_spec = pltpu.PrefetchScalarGridSpec(
        num_scalar_prefetch=2, grid=(s // tme,),
        in_specs=[pl.BlockSpec((tme, d), tile),
                  pl.BlockSpec((1, d, de), wmap), pl.BlockSpec((1, d, de), wmap), pl.BlockSpec((1, de, d), wmap)],
        out_specs=pl.BlockSpec((tme, d), lambda i, te, nt: (i, 0)),
        scratch_shapes=[pltpu.VMEM((d, de), BF16), pltpu.VMEM((d, de), BF16), pltpu.VMEM((de, d), BF16)])
    return pl.pallas_call(
        _expert_kernel, grid_spec=grid_spec,
        out_shape=jax.ShapeDtypeStruct((s, d), F32),
        compiler_params=_cparams(("arbitrary",)),
        name="experts")(tile_expert, n_tiles, xs, wg, wu, wd)


def _combine_kernel(nt, slot_ref, slot_next_ref, x1_ref, route_ref, gf_ref, ys_hbm, y_ref, buf_ref, sem):
    tmf = x1_ref.shape[0]
    i = pl.program_id(0)
    cur = i % 2

    def gather(s_ref, half):
        for r in range(tmf):
            for k in range(2):
                pltpu.make_async_copy(ys_hbm.at[pl.ds(s_ref[0, 0, 2 * r + k], 1)],
                                      buf_ref.at[half, k, pl.ds(r, 1)], sem.at[half]).start(priority=k)

    @pl.when(i == 0)
    def _():
        gather(slot_ref, 0)

    @pl.when(i + 1 < nt)
    def _():
        gather(slot_next_ref, 1 - cur)

    for k in range(2):
        pltpu.make_async_copy(ys_hbm.at[pl.ds(0, tmf)], buf_ref.at[cur, k], sem.at[cur]).wait()
    w1 = route_ref[:, 2:3]
    w2 = route_ref[:, 3:4]
    x2 = x1_ref[...] + (w1 * buf_ref[cur, 0] + w2 * buf_ref[cur, 1])
    y_ref[...] = _rms(x2, gf_ref[...])


def _combine(slots, x1, route, gf, ys, tmf):
    n, d = x1.shape
    nt = n // tmf
    row = lambda i: (i, 0)
    slots3 = slots.reshape(nt, 1, 2 * tmf)
    return pl.pallas_call(
        functools.partial(_combine_kernel, nt), grid=(nt,),
        in_specs=[pl.BlockSpec((1, 1, 2 * tmf), lambda i: (i, 0, 0), memory_space=pltpu.SMEM),
                  pl.BlockSpec((1, 1, 2 * tmf), lambda i: (jnp.minimum(i + 1, nt - 1), 0, 0),
                               memory_space=pltpu.SMEM),
                  pl.BlockSpec((tmf, d), row), pl.BlockSpec((tmf, LANES), row), _full((1, d)),
                  pl.BlockSpec(memory_space=pl.ANY)],
        out_specs=pl.BlockSpec((tmf, d), row),
        out_shape=jax.ShapeDtypeStruct((n, d), F32),
        scratch_shapes=[pltpu.VMEM((2, 2, tmf, d), F32), pltpu.SemaphoreType.DMA((2,))],
        compiler_params=_cparams(("arbitrary",)),
        name="combine")(slots3, slots3, x1, route, gf, ys)


def _pick(n, pref):
    t = min(n, pref)
    while n % t:
        t //= 2
    return t


def _mixer(x3, hist, k_past_t, v_past, pos, p, n_heads, lam_init, zero_shape=None):
    b, t, d = x3.shape
    n = b * t
    x2 = x3.reshape(n, d)
    c_conv = p["w_dw"].shape[1]
    prompt = k_past_t is None
    tm = _pick(n, 512)
    tabs = _rope_tables(pos, max(t, tm))
    q_scale = (d // n_heads // 4) ** -0.5 * LOG2E
    outs = _in_proj(x2, p["norm1_g"], p["w_cat"], p["b_cat"], tabs, q_scale, tm, t, prompt)
    if prompt:
        glu, q, k_bf, k_t, v, v_t, gates = outs
    else:
        glu, q, k, v, gates = outs

    glu3 = glu.reshape(b, t, c_conv)
    gates3 = gates.reshape(b, t, 2 * d)
    gc = _conv_branch(glu3, hist, p["w_dw"], p["b_dw"], p["ln_g"], p["ln_b"], p["w_co"], p["b_co"],
                      gates3, _pick(t, 512))

    w = q.shape[1]
    q3 = q.reshape(b, t, w)
    hd = w // n_heads // 2
    if prompt:
        o, *zbuf = _attn_prompt(q3, k_bf.reshape(b, t, w), v_t, p["lam4"], p["subln_g"].reshape(-1, 1), lam_init,
                                2 * KV_BLOCK, zero_shape)
        k_out = k_t.reshape(b, n_heads, 2, hd, t).transpose(0, 4, 1, 2, 3)[None]
    else:
        k3 = k.reshape(b, t, w)
        o = _attn_sample(q3, k_past_t, v_past, k3, v.reshape(b, t, w), p["lam4"], p["subln_g"], lam_init, n_heads)
        k_out = k3.reshape(1, b, t, n_heads, 2, hd)
    v_out = v.reshape(1, b, t, n_heads, w // n_heads)

    x1, route = _mid(x2, gc.reshape(n, d), gates, o.reshape(n, w), p["w_ao"], p["w_out"],
                     p["norm2_g"], p["wr2"], p["wrh"], p["br"], p["n_groups"], p["epg"], _pick(n, 512))
    new_hist = glu3[:, t - (p["w_dw"].shape[0] - 1):, :]
    return x1, route, k_out, v_out, new_hist[None], (zbuf[0] if prompt and zbuf else None)


def kernel(x_prompt, x_sample, cache_k, cache_v, state_conv, norm1_g, w_in, b_glu, w_dw, b_dw, conv_ln_g, conv_ln_b, w_conv_out, b_conv_out, lambda_q1, lambda_k1, lambda_q2, lambda_k2, subln_g, w_attn_out, w_out, norm2_g, w_router_group, b_router_group, w_router_expert, b_router_expert, w_exp_gate, w_exp_up, w_exp_down, final_norm_g):
    bp, tp, d = x_prompt.shape
    bs, ts, _ = x_sample.shape
    assert w_in.shape[0] == 1
    l = 0
    past = cache_k.shape[2]
    n_heads = cache_k.shape[3]
    hd = cache_k.shape[5]
    v_dim = cache_v.shape[4]
    conv_w = w_dw.shape[1]
    c_conv = w_dw.shape[2]
    n_groups = w_router_group.shape[2]
    n_exp = w_router_expert.shape[2]
    epg = n_exp // n_groups
    lam_init = 0.8 - 0.6 * math.exp(-0.3 * l)
    assert c_conv == 1024 and n_heads * 2 * hd == 1024 and n_heads * v_dim == 1024 and tp % (2 * KV_BLOCK) == 0

    hc = c_conv // 2
    wl = w_in[l]
    w_cat = jnp.concatenate([wl[:, :hc], wl[:, c_conv:c_conv + hc], wl[:, hc:c_conv], wl[:, c_conv + hc:]],
                            axis=1).astype(BF16)
    bg = b_glu[l]
    b_cat = jnp.concatenate([bg[:hc], bg[c_conv:c_conv + hc], bg[hc:c_conv], bg[c_conv + hc:]])[None]

    wr = jnp.zeros((d, LANES), F32)
    wr = wr.at[:, :n_exp].set(w_router_expert[l]).at[:, n_exp:n_exp + n_groups].set(w_router_group[l])
    wrh = wr.astype(BF16)
    wrl = (wr - wrh.astype(F32)).astype(BF16)
    br = jnp.zeros((1, LANES), F32)
    br = br.at[0, :n_exp].set(b_router_expert[l]).at[0, n_exp:n_exp + n_groups].set(b_router_group[l])
    p = dict(
        norm1_g=norm1_g[l][None], w_cat=w_cat, b_cat=b_cat,
        w_dw=w_dw[l], b_dw=b_dw[l][None], ln_g=conv_ln_g[l][None], ln_b=conv_ln_b[l][None],
        w_co=w_conv_out[l].astype(BF16), b_co=b_conv_out[l][None],
        lam4=jnp.stack([lambda_q1[l], lambda_k1[l], lambda_q2[l], lambda_k2[l]]),
        subln_g=subln_g[l][None], w_ao=w_attn_out[l].astype(BF16), w_out=w_out[l].astype(BF16),
        norm2_g=norm2_g[l][None], wr2=jnp.concatenate([wrh, wrl], axis=1), wrh=wrh, br=br,
        n_groups=n_groups, epg=epg)

    hist_p = jnp.zeros((bp, HIST_PAD, c_conv), F32)
    hist_s = jnp.concatenate(
        [jnp.zeros((bs, HIST_PAD - (conv_w - 1), c_conv), F32), state_conv[l]], axis=1)
    ckt = cache_k[l].transpose(0, 2, 3, 4, 1).reshape(bs, n_heads * 2 * hd, past)
    cv = cache_v[l].reshape(bs, past * n_heads, v_dim)

    np_, ns_ = bp * tp, bs * ts
    n = np_ + ns_
    tme = 256
    max_tiles = (2 * n) // tme + n_exp
    xs_rows = max_tiles * tme
    attn_steps = bp * n_heads * (tp // (2 * KV_BLOCK))
    fused_zero = xs_rows % attn_steps == 0 and (xs_rows // attn_steps) % SUBLANES == 0

    x1p, routep, kp, vp, cp, xs = _mixer(x_prompt, hist_p, None, None, jnp.arange(tp), p, n_heads, lam_init,
                                         (xs_rows, d) if fused_zero else None)
    x1s, routes, ks, vs, cs, _ = _mixer(x_sample, hist_s, ckt, cv, past + jnp.arange(ts), p, n_heads, lam_init)
    if xs is None:
        xs = jnp.zeros((xs_rows, d), F32)

    e_idx = jnp.concatenate([routep[:, :2], routes[:, :2]], axis=0).astype(I32)
    onehot = (e_idx[:, :, None] == jnp.arange(n_exp, dtype=I32)[None, None, :]).astype(I32)
    per_tok = onehot.sum(axis=1)
    before = jnp.cumsum(per_tok, axis=0) - per_tok
    rank = jnp.sum(onehot * before[:, None, :], axis=-1)
    counts = per_tok.sum(axis=0)
    tiles_per = (counts + tme - 1) // tme
    tile_end = jnp.cumsum(tiles_per)
    tile_start = tile_end - tiles_per
    slots = (jnp.sum(onehot * (tile_start * tme)[None, None, :], axis=-1) + rank).astype(I32)
    tile_ids = jnp.arange(max_tiles, dtype=I32)
    tile_expert = jnp.minimum(jnp.sum(tile_ids[:, None] >= tile_end[None, :], axis=1), n_exp - 1).astype(I32)
    n_tiles = tile_end[-1:].astype(I32)

    g2 = norm2_g[l][None]
    xs = _dispatch(slots[:np_].reshape(-1), x1p, g2, xs, _pick(np_, 256))
    xs = _dispatch(slots[np_:].reshape(-1), x1s, g2, xs, _pick(ns_, 256))
    ys = _experts(tile_expert, n_tiles, xs, w_exp_gate[l], w_exp_up[l], w_exp_down[l], tme)
    gf = final_norm_g[None]
    yp = _combine(slots[:np_].reshape(-1), x1p, routep, gf, ys, _pick(np_, 256))
    ysm = _combine(slots[np_:].reshape(-1), x1s, routes, gf, ys, _pick(ns_, 256))

    return (yp.reshape(bp, tp, d), ysm.reshape(bs, ts, d), kp, vp, cp, ks, vs, cs)
```
